```python
import math
import jax, jax.numpy as jnp
from jax import lax
import numpy as np


D_MODEL = 1024
BATCH = 4
SEQ = 8192
DEPTH = 2

GRID_W = 64
CTX_LEN = 256
MIX_W = D_MODEL
DA_HEAD_DIM = 64
DA_W = MIX_W // 2
DA_HEADS = DA_W // (2 * DA_HEAD_DIM)
FN_W = MIX_W - DA_W
FN_GROUPS = 4
FN_GW = FN_W // FN_GROUPS
Q_BLOCK = 128
ROPE_THETA = 10000.0
ROPE_AXIS_DIM = DA_HEAD_DIM // 2
HY_FILTER_EMB = 33
HY_FILTER_HIDDEN = 64
HY_DECAY_TARGET = 1e-2
HY_FAST_DECAY = 0.3
HY_SLOW_DECAY = 1.5
FFN_HIDDEN = ((8 * D_MODEL // 3 + 127) // 128) * 128
N_MOD = 6
LN_EPS = 1e-5
F32 = jnp.float32

kernel_name = 'hybrid_diffattn_fnet_hyena_convffn_dit'


def layer_norm(x, g=None, b=None):
    xf = x.astype(F32)
    mu = jnp.mean(xf, -1, keepdims=True)
    var = jnp.mean(jnp.square(xf - mu), -1, keepdims=True)
    y = (xf - mu) * lax.rsqrt(var + LN_EPS)
    if g is not None:
        y = y * g.astype(F32) + b.astype(F32)
    return y.astype(x.dtype)


def ada_params(cvec, w, b):
    m = jax.nn.silu(cvec) @ w + b
    return jnp.split(m[..., None, :], N_MOD, axis=-1)


def modulate(x, shift, scale):
    return layer_norm(x) * (1 + scale) + shift


def dwconv3(u, w, b):
    up = jnp.pad(u, ((0, 0), (1, 1), (0, 0)))
    return up[:, :-2] * w[0] + up[:, 1:-1] * w[1] + up[:, 2:] * w[2] + b


def axial_rope_angles(rows):
    pos_r = jnp.repeat(jnp.arange(rows, dtype=F32), GRID_W)
    pos_c = jnp.tile(jnp.arange(GRID_W, dtype=F32), rows)
    inv = ROPE_THETA ** (-jnp.arange(0, ROPE_AXIS_DIM, 2, dtype=F32) / ROPE_AXIS_DIM)
    return jnp.concatenate([pos_r[:, None] * inv, pos_c[:, None] * inv], -1)


def _rotate(xs, ang):
    x1, x2 = jnp.split(xs, 2, axis=-1)
    cos = jnp.cos(ang)[None, :, None, :].astype(xs.dtype)
    sin = jnp.sin(ang)[None, :, None, :].astype(xs.dtype)
    return jnp.concatenate([x1 * cos - x2 * sin, x2 * cos + x1 * sin], -1)


def rope_2d(x, ang):
    n = ang.shape[-1] // 2
    xr, xc = jnp.split(x, 2, axis=-1)
    return jnp.concatenate([_rotate(xr, ang[:, :n]), _rotate(xc, ang[:, n:])], -1)


def lambda_init(layer_idx):
    return 0.8 - 0.6 * math.exp(-0.3 * layer_idx)


def _split_qk(t):
    B, L, _ = t.shape
    t = t.reshape(B, L, DA_HEADS, 2, DA_HEAD_DIM)
    return t[..., 0, :], t[..., 1, :]


def diff_attention(q1, q2, k1, k2, v, lam):
    B, Lq, H, d = q1.shape
    nb = Lq // Q_BLOCK
    scale = d ** -0.5

    def to_blocks(q):
        return q.reshape(B, nb, Q_BLOCK, H, d).transpose(1, 0, 2, 3, 4)

    def block(qs):
        a1, a2 = qs
        s1 = jnp.einsum('bqhd,bkhd->bhqk', a1, k1).astype(F32) * scale
        s2 = jnp.einsum('bqhd,bkhd->bhqk', a2, k2).astype(F32) * scale
        p = jax.nn.softmax(s1, axis=-1) - lam * jax.nn.softmax(s2, axis=-1)
        return jnp.einsum('bhqk,bkhe->bqhe', p.astype(v.dtype), v)

    o = lax.map(block, (to_blocks(q1), to_blocks(q2)))
    return o.transpose(1, 0, 2, 3, 4).reshape(B, Lq, H, v.shape[-1])


def head_rmsnorm(o, g, post):
    of = o.astype(F32)
    y = of * lax.rsqrt(jnp.mean(of * of, -1, keepdims=True) + LN_EPS) * g.astype(F32) * post
    return y.astype(o.dtype)


def fourier_mix(f):
    B, L, _ = f.shape
    g = f.astype(F32).reshape(B, L, FN_GROUPS, FN_GW)
    return jnp.fft.fftn(g, axes=(1, 3), norm='ortho').real.reshape(B, L, FN_W).astype(f.dtype)


def diff_fourier_mixer(h_lat, h_ctx, w_in, w_out, lq1, lk1, lq2, lk2, subln_g, lam_init, ang, ctx_out):
    B, L, _ = h_lat.shape
    Lc = h_ctx.shape[1]
    lam = (jnp.exp(jnp.sum(lq1.astype(F32) * lk1.astype(F32)))
           - jnp.exp(jnp.sum(lq2.astype(F32) * lk2.astype(F32))) + lam_init)
    post = 1.0 - lam_init
    kv_c = h_ctx @ w_in[:, DA_W:3 * DA_W]
    k1c, k2c = _split_qk(kv_c[..., :DA_W])
    v_c = kv_c[..., DA_W:].reshape(B, Lc, DA_HEADS, 2 * DA_HEAD_DIM)
    p = h_lat @ w_in
    q, k, v, f = jnp.split(p, [DA_W, 2 * DA_W, 3 * DA_W], axis=-1)
    q1, q2 = _split_qk(q)
    k1, k2 = _split_qk(k)
    q1, q2, k1, k2 = rope_2d(q1, ang), rope_2d(q2, ang), rope_2d(k1, ang), rope_2d(k2, ang)
    v = v.reshape(B, L, DA_HEADS, 2 * DA_HEAD_DIM)
    K1 = jnp.concatenate([k1c, k1], axis=1)
    K2 = jnp.concatenate([k2c, k2], axis=1)
    V = jnp.concatenate([v_c, v], axis=1)
    o = head_rmsnorm(diff_attention(q1, q2, K1, K2, V, lam), subln_g, post).reshape(B, L, DA_W)
    y_lat = jnp.concatenate([o, fourier_mix(f)], axis=-1) @ w_out
    y_ctx = None
    if ctx_out:
        q1c, q2c = _split_qk(h_ctx @ w_in[:, :DA_W])
        oc = head_rmsnorm(diff_attention(q1c, q2c, k1c, k2c, v_c, lam), subln_g, post).reshape(B, Lc, DA_W)
        y_ctx = jnp.concatenate([oc, fourier_mix(h_ctx @ w_in[:, 3 * DA_W:])], axis=-1) @ w_out
    return y_lat, y_ctx


def hyena_filter_freq(L, w1, b1, w2, b2, w3, b3, freq, w4):
    t = jnp.linspace(0.0, 1.0, L, dtype=F32)[:, None]
    bands = (HY_FILTER_EMB - 1) // 2
    w = 2.0 * math.pi * jnp.arange(L, dtype=F32)[:, None] / L
    fb = jnp.linspace(1e-4, bands - 1, bands, dtype=F32)[None, :]
    z = jnp.concatenate([t, jnp.cos(fb * w), -jnp.sin(fb * w)], axis=-1)
    fr = freq.astype(F32)
    hdn = jnp.sin(fr * (z @ w1.astype(F32) + b1.astype(F32)))
    hdn = jnp.sin(fr * (hdn @ w2.astype(F32) + b2.astype(F32)))
    hdn = jnp.sin(fr * (hdn @ w3.astype(F32) + b3.astype(F32)))
    h = (hdn @ w4.astype(F32)).reshape(L, 2, D_MODEL)
    min_decay = math.log(HY_DECAY_TARGET) / HY_SLOW_DECAY
    max_decay = math.log(HY_DECAY_TARGET) / HY_FAST_DECAY
    deltas = jnp.abs(jnp.linspace(min_decay, max_decay, D_MODEL, dtype=F32))
    h = h * jnp.exp(-t * deltas)[:, None, :]
    hf, hb = h[:, 0], h[:, 1]
    k2 = jnp.concatenate([hf, jnp.zeros((1, D_MODEL), F32), hb[1:][::-1]], axis=0)
    k2 = k2 / jnp.sum(jnp.abs(k2), axis=0, keepdims=True)
    return jnp.fft.rfft(k2, axis=0)


def hyena_mixer(h, w_in, conv_w, conv_b, filt, d_skip, w_out):
    L = h.shape[1]
    u = dwconv3(h @ w_in, conv_w, conv_b)
    x0, x1, v = jnp.split(u, 3, axis=-1)
    v = v * x1
    k_f = hyena_filter_freq(L, *filt)
    vf = v.astype(F32)
    y = jnp.fft.irfft(jnp.fft.rfft(vf, n=2 * L, axis=1) * k_f[None], n=2 * L, axis=1)[:, :L]
    y = (y + vf * d_skip.astype(F32)).astype(h.dtype)
    return (y * x0) @ w_out


def conv_ffn(h, w_in, conv_w, conv_b, w_out):
    u = dwconv3(h @ w_in, conv_w, conv_b)
    a, g = jnp.split(u, 2, axis=-1)
    return (jax.nn.gelu(a, approximate=False) * g) @ w_out


def setup_inputs(seed: int = 0) -> dict:
    key = jax.random.key(seed)
    ks = jax.random.split(key, 34)
    n_even = (DEPTH + 1) // 2
    n_odd = DEPTH // 2
    beta = (8 * DEPTH) ** -0.25
    D = D_MODEL
    F = FFN_HIDDEN
    H = HY_FILTER_HIDDEN

    def nrm(k, shape, s):
        return jax.random.normal(k, shape, F32) * s

    return {
        'x': nrm(ks[0], (BATCH, SEQ, D), 1.0),
        'c': nrm(ks[1], (BATCH, D), 1.0),
        'ctx': nrm(ks[2], (BATCH, CTX_LEN, D), 1.0),
        'c_ctx': nrm(ks[3], (D,), 1.0),
        'mod_w': nrm(ks[4], (DEPTH, D, N_MOD * D), D ** -0.5),
        'mod_b': nrm(ks[5], (DEPTH, N_MOD * D), 0.02),
        'ln1_g': 1.0 + nrm(ks[6], (DEPTH, D), 0.02),
        'ln1_b': nrm(ks[7], (DEPTH, D), 0.02),
        'ln2_g': 1.0 + nrm(ks[8], (DEPTH, D), 0.02),
        'ln2_b': nrm(ks[9], (DEPTH, D), 0.02),
        'ffn_w_in': nrm(ks[10], (DEPTH, D, 2 * F), D ** -0.5),
        'ffn_conv_w': nrm(ks[11], (DEPTH, 3, 2 * F), 3 ** -0.5),
        'ffn_conv_b': nrm(ks[12], (DEPTH, 2 * F), 0.02),
        'ffn_w_out': nrm(ks[13], (DEPTH, F, D), beta * F ** -0.5),
        'da_w_in': nrm(ks[14], (n_even, D, 3 * DA_W + FN_W), D ** -0.5),
        'da_w_out': nrm(ks[15], (n_even, MIX_W, D), beta * MIX_W ** -0.5),
        'da_lam_q1': nrm(ks[16], (n_even, DA_HEAD_DIM), 0.1),
        'da_lam_k1': nrm(ks[17], (n_even, DA_HEAD_DIM), 0.1),
        'da_lam_q2': nrm(ks[18], (n_even, DA_HEAD_DIM), 0.1),
        'da_lam_k2': nrm(ks[19], (n_even, DA_HEAD_DIM), 0.1),
        'da_subln_g': 1.0 + nrm(ks[20], (n_even, 2 * DA_HEAD_DIM), 0.02),
        'hy_w_in': nrm(ks[21], (n_odd, D, 3 * MIX_W), D ** -0.5),
        'hy_conv_w': nrm(ks[22], (n_odd, 3, 3 * MIX_W), 3 ** -0.5),
        'hy_conv_b': nrm(ks[23], (n_odd, 3 * MIX_W), 0.02),
        'hy_f_w1': nrm(ks[24], (n_odd, HY_FILTER_EMB, H), HY_FILTER_EMB ** -0.5),
        'hy_f_b1': nrm(ks[25], (n_odd, H), 0.1),
        'hy_f_w2': nrm(ks[26], (n_odd, H, H), H ** -0.5),
        'hy_f_b2': nrm(ks[27], (n_odd, H), 0.1),
        'hy_f_w3': nrm(ks[28], (n_odd, H, H), H ** -0.5),
        'hy_f_b3': nrm(ks[29], (n_odd, H), 0.1),
        'hy_f_freq': 1.0 + nrm(ks[30], (n_odd, H), 0.1),
        'hy_f_w4': nrm(ks[31], (n_odd, H, 2 * MIX_W), H ** -0.5),
        'hy_d': nrm(ks[32], (n_odd, MIX_W), 1.0),
        'hy_w_out': nrm(ks[33], (n_odd, MIX_W, D), beta * MIX_W ** -0.5),
    }


def reference(x, c, ctx, c_ctx, mod_w, mod_b, ln1_g, ln1_b, ln2_g, ln2_b,
              ffn_w_in, ffn_conv_w, ffn_conv_b, ffn_w_out,
              da_w_in, da_w_out, da_lam_q1, da_lam_k1, da_lam_q2, da_lam_k2, da_subln_g,
              hy_w_in, hy_conv_w, hy_conv_b, hy_f_w1, hy_f_b1, hy_f_w2, hy_f_b2,
              hy_f_w3, hy_f_b3, hy_f_freq, hy_f_w4, hy_d, hy_w_out):
    n_lat = x.shape[1]
    rows = n_lat // GRID_W
    ang = axial_rope_angles(rows)
    alpha = (2 * DEPTH) ** 0.25
    for i in range(DEPTH):
        even = i % 2 == 0
        ctx_out = any(j % 2 == 0 for j in range(i + 1, DEPTH))
        sh1, sc1, g1, sh2, sc2, g2 = ada_params(c, mod_w[i], mod_b[i])
        h_lat = modulate(x, sh1, sc1)
        if even or ctx_out:
            csh1, csc1, cg1, csh2, csc2, cg2 = ada_params(c_ctx, mod_w[i], mod_b[i])
            h_ctx = modulate(ctx, csh1, csc1)
        if even:
            e = i // 2
            y_lat, y_ctx = diff_fourier_mixer(h_lat, h_ctx, da_w_in[e], da_w_out[e],
                                              da_lam_q1[e], da_lam_k1[e], da_lam_q2[e], da_lam_k2[e],
                                              da_subln_g[e], lambda_init(i), ang, ctx_out)
        else:
            o = i // 2
            filt = (hy_f_w1[o], hy_f_b1[o], hy_f_w2[o], hy_f_b2[o], hy_f_w3[o], hy_f_b3[o],
                    hy_f_freq[o], hy_f_w4[o])
            y_lat = hyena_mixer(h_lat, hy_w_in[o], hy_conv_w[o], hy_conv_b[o], filt, hy_d[o], hy_w_out[o])
            if ctx_out:
                y_ctx = hyena_mixer(h_ctx, hy_w_in[o], hy_conv_w[o], hy_conv_b[o], filt, hy_d[o], hy_w_out[o])
        ffn = (ffn_w_in[i], ffn_conv_w[i], ffn_conv_b[i], ffn_w_out[i])
        x = layer_norm(alpha * x + g1 * y_lat, ln1_g[i], ln1_b[i])
        x = layer_norm(alpha * x + g2 * conv_ffn(modulate(x, sh2, sc2), *ffn), ln2_g[i], ln2_b[i])
        if ctx_out:
            ctx = layer_norm(alpha * ctx + cg1 * y_ctx, ln1_g[i], ln1_b[i])
            ctx = layer_norm(alpha * ctx + cg2 * conv_ffn(modulate(ctx, csh2, csc2), *ffn), ln2_g[i], ln2_b[i])
    return x
```

```python
import functools
import math

import numpy as np
import jax
import jax.numpy as jnp
from jax import lax
from jax.experimental import pallas as pl
from jax.experimental.pallas import tpu as pltpu

F32 = jnp.float32
BF16 = jnp.bfloat16

GRID_W = 64
DA_HEAD_DIM = 64
ROPE_THETA = 10000.0
N_MOD = 6
LN_EPS = 1e-5
FN_GROUPS = 4
HY_FILTER_EMB = 33
HY_DECAY_TARGET = 1e-2
HY_FAST_DECAY = 0.3
HY_SLOW_DECAY = 1.5
DFT_INNER = GRID_W

V7X_LANES = 128
V7X_VMEM_LIMIT_BYTES = 56 * 1024 * 1024
HALO = 8


def _cp(*sem):
    return pltpu.CompilerParams(dimension_semantics=sem, vmem_limit_bytes=V7X_VMEM_LIMIT_BYTES)


def _dot(a, b):
    return jnp.dot(a, b, preferred_element_type=F32)


def _split(a):
    hi = a.astype(BF16)
    lo = (a - hi.astype(F32)).astype(BF16)
    return hi, lo


def _dot3(a, b):
    ah, al = _split(a)
    bh, bl = _split(b)
    return _dot(ah, bh) + _dot(al, bh) + _dot(ah, bl)


def _ln(x):
    mu = jnp.mean(x, axis=-1, keepdims=True)
    xc = x - mu
    var = jnp.mean(xc * xc, axis=-1, keepdims=True)
    return xc * lax.rsqrt(var + LN_EPS)


def _pick(n, prefs):
    for p in prefs:
        if n % p == 0:
            return p
    return n


def _mod_kernel(cv_ref, w_ref, b_ref, o_ref):
    cv = cv_ref[...]
    s = cv * jax.nn.sigmoid(cv)
    o_ref[0] = _dot3(s, w_ref[0]) + b_ref[0]


def _mod_params(cv, mod_w, mod_b):
    depth, d, n = mod_w.shape
    rows = cv.shape[0]
    tn = _pick(n, (1536, 1024, 512, 256, 128))
    return pl.pallas_call(
        _mod_kernel,
        grid=(depth, n // tn),
        in_specs=[pl.BlockSpec((rows, d), lambda i, j: (0, 0)),
                  pl.BlockSpec((1, d, tn), lambda i, j: (i, 0, j)),
                  pl.BlockSpec((1, 1, tn), lambda i, j: (i, 0, j))],
        out_specs=pl.BlockSpec((1, rows, tn), lambda i, j: (i, 0, j)),
        out_shape=jax.ShapeDtypeStruct((depth, rows, n), F32),
        compiler_params=_cp("parallel", "parallel"),
        name="mod_params",
    )(cv, mod_w, mod_b.reshape(depth, 1, n))


def _rope_block(blk, cos, sin_a, sin_b):
    return (blk * cos + pltpu.roll(blk, V7X_LANES - 16, 1) * sin_a + pltpu.roll(blk, 16, 1) * sin_b)


def _inproj_kernel(x_ref, sh_ref, sc_ref, w_ref, cos_ref, sa_ref, sb_ref,
                   q_ref, k_ref, v_ref, f_ref, *, da_w, q_scale):
    h = (_ln(x_ref[0]) * (1.0 + sc_ref[0]) + sh_ref[0]).astype(BF16)
    p = _dot(h, w_ref[...])
    cos, sa, sb = cos_ref[...], sa_ref[...], sb_ref[...]
    for j in range(da_w // V7X_LANES):
        lo, hi = j * V7X_LANES, (j + 1) * V7X_LANES
        q_ref[0, :, lo:hi] = (_rope_block(p[:, lo:hi], cos, sa, sb) * q_scale).astype(BF16)
        k_ref[0, :, lo:hi] = _rope_block(p[:, da_w + lo:da_w + hi], cos, sa, sb).astype(BF16)
    v_ref[0] = p[:, 2 * da_w:3 * da_w].astype(BF16)
    f_ref[0] = p[:, 3 * da_w:].astype(BF16)


def _inproj(x, sh, sc, w, cos, sa, sb, da_w):
    b, l, d = x.shape
    n = w.shape[1]
    fn_w = n - 3 * da_w
    tm = _pick(l, (512, 256, 128))
    row = lambda bi, i: (bi, i, 0)
    vec = lambda bi, i: (bi, 0, 0)
    tab = lambda bi, i: (i, 0)
    return pl.pallas_call(
        functools.partial(_inproj_kernel, da_w=da_w, q_scale=DA_HEAD_DIM ** -0.5),
        grid=(b, l // tm),
        in_specs=[pl.BlockSpec((1, tm, d), row),
                  pl.BlockSpec((1, 1, d), vec), pl.BlockSpec((1, 1, d), vec),
                  pl.BlockSpec((d, n), lambda bi, i: (0, 0)),
                  pl.BlockSpec((tm, V7X_LANES), tab), pl.BlockSpec((tm, V7X_LANES), tab),
                  pl.BlockSpec((tm, V7X_LANES), tab)],
        out_specs=[pl.BlockSpec((1, tm, da_w), row), pl.BlockSpec((1, tm, da_w), row),
                   pl.BlockSpec((1, tm, da_w), row), pl.BlockSpec((1, tm, fn_w), row)],
        out_shape=[jax.ShapeDtypeStruct((b, l, da_w), BF16), jax.ShapeDtypeStruct((b, l, da_w), BF16),
                   jax.ShapeDtypeStruct((b, l, da_w), BF16), jax.ShapeDtypeStruct((b, l, fn_w), BF16)],
        compiler_params=_cp("parallel", "parallel"),
        name="l0_inproj",
    )(x, sh, sc, w, cos, sa, sb)


def _lnmm_kernel(x_ref, sh_ref, sc_ref, w_ref, o_ref):
    h = (_ln(x_ref[0]) * (1.0 + sc_ref[0]) + sh_ref[0]).astype(BF16)
    o_ref[0] = _dot(h, w_ref[...]).astype(o_ref.dtype)


def _lnmm(x, sh, sc, w):
    b, l, d = x.shape
    n = w.shape[1]
    tm = _pick(l, (256, 128))
    return pl.pallas_call(
        _lnmm_kernel,
        grid=(b, l // tm),
        in_specs=[pl.BlockSpec((1, tm, d), lambda bi, i: (bi, i, 0)),
                  pl.BlockSpec((1, 1, d), lambda bi, i: (0, 0, 0)),
                  pl.BlockSpec((1, 1, d), lambda bi, i: (0, 0, 0)),
                  pl.BlockSpec((d, n), lambda bi, i: (0, 0))],
        out_specs=pl.BlockSpec((1, tm, n), lambda bi, i: (bi, i, 0)),
        out_shape=jax.ShapeDtypeStruct((b, l, n), BF16),
        compiler_params=_cp("parallel", "parallel"),
        name="ctx_kv_proj",
    )(x, sh, sc, w)


def _attn_kernel(q_ref, kc_ref, vc_ref, k_ref, v_ref, lam_ref, g_ref, o_ref,
                 m1_ref, l1_ref, a1_ref, m2_ref, l2_ref, a2_ref, *, tk, lam_init):
    q = q_ref[0]
    lane = lax.broadcasted_iota(jnp.int32, q.shape, 1)
    zero = jnp.zeros_like(q)
    q1 = jnp.where(lane < DA_HEAD_DIM, q, zero)
    q2 = jnp.where(lane >= DA_HEAD_DIM, q, zero)
    nt = (((1,), (1,)), ((), ()))

    m1_ref[...] = jnp.full_like(m1_ref, -jnp.inf)
    m2_ref[...] = jnp.full_like(m2_ref, -jnp.inf)
    l1_ref[...] = jnp.zeros_like(l1_ref)
    l2_ref[...] = jnp.zeros_like(l2_ref)
    a1_ref[...] = jnp.zeros_like(a1_ref)
    a2_ref[...] = jnp.zeros_like(a2_ref)

    def branch(qm, kb, vb, m_ref, l_ref, a_ref):
        s = lax.dot_general(qm, kb, nt, preferred_element_type=F32)
        m_old = m_ref[...]
        m_new = jnp.maximum(m_old, jnp.max(s, axis=-1, keepdims=True))
        p = jnp.exp(s - m_new)
        alpha = jnp.exp(m_old - m_new)
        l_ref[...] = alpha * l_ref[...] + jnp.sum(p, axis=-1, keepdims=True)
        a_ref[...] = alpha * a_ref[...] + _dot(p.astype(BF16), vb)
        m_ref[...] = m_new

    def step(kb, vb):
        branch(q1, kb, vb, m1_ref, l1_ref, a1_ref)
        branch(q2, kb, vb, m2_ref, l2_ref, a2_ref)

    step(kc_ref[0], vc_ref[0])

    def body(i, carry):
        off = pl.multiple_of(i * tk, tk)
        step(k_ref[0, pl.ds(off, tk), :], v_ref[0, pl.ds(off, tk), :])
        return carry

    lax.fori_loop(0, k_ref.shape[1] // tk, body, 0)

    lp = lam_ref[...]
    lam = (jnp.exp(jnp.sum(lp[0:1] * lp[1:2], axis=-1, keepdims=True))
           - jnp.exp(jnp.sum(lp[2:3] * lp[3:4], axis=-1, keepdims=True)) + lam_init)
    o = a1_ref[...] / l1_ref[...] - lam * (a2_ref[...] / l2_ref[...])
    ms = jnp.mean(o * o, axis=-1, keepdims=True)
    o_ref[0] = (o * lax.rsqrt(ms + LN_EPS) * g_ref[...] * (1.0 - lam_init)).astype(o_ref.dtype)


def _diff_attention(q, k, v, kvc, lam_rows, subln_g, lam_init):
    b, l, da_w = q.shape
    lc = kvc.shape[1]
    hw = 2 * DA_HEAD_DIM
    heads = da_w // hw
    tq = _pick(l, (512, 256, 128))
    tk = _pick(l, (512, 256, 128))
    qmap = lambda bi, h, i: (bi, i, h)
    kmap = lambda bi, h, i: (bi, 0, h)
    return pl.pallas_call(
        functools.partial(_attn_kernel, tk=tk, lam_init=lam_init),
        grid=(b, heads, l // tq),
        in_specs=[pl.BlockSpec((1, tq, hw), qmap),
                  pl.BlockSpec((1, lc, hw), kmap),
                  pl.BlockSpec((1, lc, hw), lambda bi, h, i: (bi, 0, heads + h)),
                  pl.BlockSpec((1, l, hw), kmap),
                  pl.BlockSpec((1, l, hw), kmap),
                  pl.BlockSpec((8, V7X_LANES), lambda bi, h, i: (0, 0)),
                  pl.BlockSpec((1, hw), lambda bi, h, i: (0, 0))],
        out_specs=pl.BlockSpec((1, tq, hw), qmap),
        out_shape=jax.ShapeDtypeStruct((b, l, da_w), BF16),
        scratch_shapes=[pltpu.VMEM((tq, 1), F32), pltpu.VMEM((tq, 1), F32), pltpu.VMEM((tq, hw), F32),
                        pltpu.VMEM((tq, 1), F32), pltpu.VMEM((tq, 1), F32), pltpu.VMEM((tq, hw), F32)],
        compiler_params=_cp("parallel", "parallel", "parallel"),
        name="diff_attention",
    )(q, kvc, kvc, k, v, lam_rows, subln_g)


def _fourier_tables(l, fn_w):
    r = l // DFT_INNER
    gw = fn_w // FN_GROUPS
    cc = np.arange(gw)
    th = 2.0 * np.pi * np.outer(cc, cc) / gw
    eye = np.eye(FN_GROUPS)
    wc = np.kron(eye, np.cos(th) / np.sqrt(gw))
    ws = np.kron(eye, -np.sin(th) / np.sqrt(gw))
    k1 = np.arange(r)[None, :, None]
    n1 = np.arange(r)[None, None, :]
    n2 = np.arange(DFT_INNER)[:, None, None]
    th1 = 2.0 * np.pi * ((k1 * (DFT_INNER * n1 + n2)) % l) / l
    gr, gi = np.cos(th1), -np.sin(th1)
    g = np.concatenate([np.concatenate([gr, -gi], 2), np.concatenate([gi, gr], 2)], 1)
    k2 = np.arange(DFT_INNER)
    th3 = 2.0 * np.pi * np.outer(k2, k2) / DFT_INNER
    m3 = np.concatenate([np.cos(th3), np.sin(th3)], 1) / np.sqrt(l)
    as_bf = lambda a: jnp.asarray(a, dtype=F32).astype(BF16)
    return as_bf(wc), as_bf(ws), as_bf(g), as_bf(m3)


def _fm1_kernel(x_ref, wc_ref, ws_ref, g_ref, o_ref, *, r):
    x = x_ref[0]
    zs = jnp.concatenate([_dot(x, wc_ref[...]), _dot(x, ws_ref[...])], axis=0).astype(BF16)
    a = _dot(g_ref[0], zs)
    o_ref[0, 0, 0] = a[:r]
    o_ref[0, 1, 0] = a[r:]


def _fm2_kernel(a_ref, m3_ref, o_ref):
    o_ref[0] = _dot(m3_ref[...], a_ref[0].astype(BF16)).astype(o_ref.dtype)


def _fourier_mix(f, tables):
    b, l, fn_w = f.shape
    r = l // DFT_INNER
    wc, ws, g, m3 = tables
    a = pl.pallas_call(
        functools.partial(_fm1_kernel, r=r),
        grid=(b, DFT_INNER),
        in_specs=[pl.BlockSpec((1, r, fn_w), lambda bi, j: (bi, 0, j)),
                  pl.BlockSpec((fn_w, fn_w), lambda bi, j: (0, 0)),
                  pl.BlockSpec((fn_w, fn_w), lambda bi, j: (0, 0)),
                  pl.BlockSpec((1, 2 * r, 2 * r), lambda bi, j: (j, 0, 0))],
        out_specs=pl.BlockSpec((1, 2, 1, r, fn_w), lambda bi, j: (bi, 0, j, 0, 0)),
        out_shape=jax.ShapeDtypeStruct((b, 2, DFT_INNER, r, fn_w), F32),
        compiler_params=_cp("parallel", "parallel"),
        name="fourier_stage1",
    )(f.reshape(b, r, DFT_INNER * fn_w), wc, ws, g)
    cols = r * fn_w
    tc = _pick(cols, (8192, 4096, 2048, 1024, 512))
    out = pl.pallas_call(
        _fm2_kernel,
        grid=(b, cols // tc),
        in_specs=[pl.BlockSpec((1, 2 * DFT_INNER, tc), lambda bi, j: (bi, 0, j)),
                  pl.BlockSpec((DFT_INNER, 2 * DFT_INNER), lambda bi, j: (0, 0))],
        out_specs=pl.BlockSpec((1, DFT_INNER, tc), lambda bi, j: (bi, 0, j)),
        out_shape=jax.ShapeDtypeStruct((b, DFT_INNER, cols), BF16),
        compiler_params=_cp("parallel", "parallel"),
        name="fourier_stage2",
    )(a.reshape(b, 2 * DFT_INNER, cols), m3)
    return out.reshape(b, l, fn_w)


def _proj_res_ln_kernel(*refs, n_in, alpha):
    ins = refs[:n_in]
    ws = refs[n_in:2 * n_in]
    x_ref, gate_ref, g_ref, b_ref, o_ref = refs[2 * n_in:]
    y = _dot(ins[0][0], ws[0][...])
    for a_ref, w_ref in zip(ins[1:], ws[1:]):
        y = y + _dot(a_ref[0], w_ref[...])
    z = alpha * x_ref[0] + gate_ref[0] * y
    o_ref[0] = _ln(z) * g_ref[...] + b_ref[...]


def _proj_res_ln(acts, weights, x, gate, ln_g, ln_b, alpha, name):
    b, l, d = x.shape
    tm = _pick(l, (512, 256, 128))
    row = lambda bi, i: (bi, i, 0)
    const = lambda bi, i: (0, 0)
    in_specs = ([pl.BlockSpec((1, tm, a.shape[2]), row) for a in acts]
                + [pl.BlockSpec(w.shape, const) for w in weights]
                + [pl.BlockSpec((1, tm, d), row), pl.BlockSpec((1, 1, d), lambda bi, i: (bi, 0, 0)),
                   pl.BlockSpec((1, d), const), pl.BlockSpec((1, d), const)])
    return pl.pallas_call(
        functools.partial(_proj_res_ln_kernel, n_in=len(acts), alpha=alpha),
        grid=(b, l // tm),
        in_specs=in_specs,
        out_specs=pl.BlockSpec((1, tm, d), row),
        out_shape=jax.ShapeDtypeStruct((b, l, d), F32),
        compiler_params=_cp("parallel", "parallel"),
        name=name,
    )(*acts, *weights, x, gate, ln_g.reshape(1, d), ln_b.reshape(1, d))


def _halo_ln(xp_ref, x_ref, xn_ref, sh_ref, sc_ref, first, last):
    mod = lambda v: _ln(v) * (1.0 + sc_ref[0]) + sh_ref[0]
    hp = mod(xp_ref[0]) * jnp.where(first, 0.0, 1.0)
    hn = mod(xn_ref[0]) * jnp.where(last, 0.0, 1.0)
    return jnp.concatenate([hp, mod(x_ref[0]), hn], axis=0).astype(BF16)


def _dwconv3(pre, cw, cb):
    n_rows = pre.shape[0]
    up = pltpu.roll(pre, 1, 0)
    dn = pltpu.roll(pre, n_rows - 1, 0)
    u = up * cw[0:1] + pre * cw[1:2] + dn * cw[2:3] + cb
    return u[HALO:n_rows - HALO]


def _halo_specs(tm, d, l):
    nb = tm // HALO
    last_blk = l // HALO - 1
    return [pl.BlockSpec((1, HALO, d), lambda bi, i, *_: (bi, jnp.maximum(i * nb - 1, 0), 0)),
            pl.BlockSpec((1, tm, d), lambda bi, i, *_: (bi, i, 0)),
            pl.BlockSpec((1, HALO, d), lambda bi, i, *_: (bi, jnp.minimum((i + 1) * nb, last_blk), 0))]


def _erf(x):
    return lax.erf(x)


def _ffn_kernel(xp_ref, x_ref, xn_ref, sh_ref, sc_ref, gate_ref,
                wa_ref, wg_ref, cwa_ref, cwg_ref, cba_ref, cbg_ref, wo_ref, g_ref, b_ref,
                o_ref, h_ref, acc_ref, *, alpha):
    i, j = pl.program_id(1), pl.program_id(2)

    @pl.when(j == 0)
    def _():
        h_ref[...] = _halo_ln(xp_ref, x_ref, xn_ref, sh_ref, sc_ref,
                              i == 0, i == pl.num_programs(1) - 1)
        acc_ref[...] = jnp.zeros_like(acc_ref)

    h = h_ref[...]
    a = _dwconv3(_dot(h, wa_ref[...]), cwa_ref[...], cba_ref[...])
    g = _dwconv3(_dot(h, wg_ref[...]), cwg_ref[...], cbg_ref[...])
    z = (0.5 * a * (1.0 + _erf(a * (2.0 ** -0.5))) * g).astype(BF16)
    acc_ref[...] += _dot(z, wo_ref[...])

    @pl.when(j == pl.num_programs(2) - 1)
    def _():
        zz = alpha * x_ref[0] + gate_ref[0] * acc_ref[...]
        o_ref[0] = _ln(zz) * g_ref[...] + b_ref[...]


def _conv_ffn(x, sh, sc, gate, w_in, conv_w, conv_b, w_out, ln_g, ln_b, alpha, name):
    b, l, d = x.shape
    f = w_out.shape[0]
    tm = _pick(l, (1024, 512, 256, 128))
    tf = _pick(f, (256, 128))
    nf = f // tf
    vec = lambda bi, i, j: (bi, 0, 0)
    const = lambda bi, i, j: (0, 0)
    cb2 = conv_b.reshape(1, 2 * f)
    return pl.pallas_call(
        functools.partial(_ffn_kernel, alpha=alpha),
        grid=(b, l // tm, nf),
        in_specs=_halo_specs(tm, d, l) + [
            pl.BlockSpec((1, 1, d), vec), pl.BlockSpec((1, 1, d), vec), pl.BlockSpec((1, 1, d), vec),
            pl.BlockSpec((d, tf), lambda bi, i, j: (0, j)),
            pl.BlockSpec((d, tf), lambda bi, i, j: (0, nf + j)),
            pl.BlockSpec((3, tf), lambda bi, i, j: (0, j)),
            pl.BlockSpec((3, tf), lambda bi, i, j: (0, nf + j)),
            pl.BlockSpec((1, tf), lambda bi, i, j: (0, j)),
            pl.BlockSpec((1, tf), lambda bi, i, j: (0, nf + j)),
            pl.BlockSpec((tf, d), lambda bi, i, j: (j, 0)),
            pl.BlockSpec((1, d), const), pl.BlockSpec((1, d), const)],
        out_specs=pl.BlockSpec((1, tm, d), lambda bi, i, j: (bi, i, 0)),
        out_shape=jax.ShapeDtypeStruct((b, l, d), F32),
        scratch_shapes=[pltpu.VMEM((tm + 2 * HALO, d), BF16), pltpu.VMEM((tm, d), F32)],
        compiler_params=_cp("parallel", "parallel", "arbitrary"),
        name=name,
    )(x, x, x, sh, sc, gate, w_in, w_in, conv_w, conv_w, cb2, cb2, w_out,
      ln_g.reshape(1, d), ln_b.reshape(1, d))


def _hy_inproj_kernel(xp_ref, x_ref, xn_ref, sh_ref, sc_ref, w_ref, cw_ref, cb_ref,
                      x0_ref, vx_ref, *, d, tn):
    i = pl.program_id(1)
    h = _halo_ln(xp_ref, x_ref, xn_ref, sh_ref, sc_ref, i == 0, i == pl.num_programs(1) - 1)
    for c in range(d // tn):
        def part(k):
            lo = k * d + c * tn
            return _dwconv3(_dot(h, w_ref[:, lo:lo + tn]), cw_ref[:, lo:lo + tn], cb_ref[:, lo:lo + tn])
        x0_ref[0, :, c * tn:(c + 1) * tn] = part(0)
        vx_ref[0, :, c * tn:(c + 1) * tn] = part(2) * part(1)


def _hy_inproj(x, sh, sc, w, conv_w, conv_b):
    b, l, d = x.shape
    tm = _pick(l, (512, 256, 128))
    tn = _pick(d, (256, 128))
    vec = lambda bi, i: (bi, 0, 0)
    const = lambda bi, i: (0, 0)
    return pl.pallas_call(
        functools.partial(_hy_inproj_kernel, d=d, tn=tn),
        grid=(b, l // tm),
        in_specs=_halo_specs(tm, d, l) + [
            pl.BlockSpec((1, 1, d), vec), pl.BlockSpec((1, 1, d), vec),
            pl.BlockSpec((d, 3 * d), const), pl.BlockSpec((3, 3 * d), const),
            pl.BlockSpec((1, 3 * d), const)],
        out_specs=[pl.BlockSpec((1, tm, d), lambda bi, i: (bi, i, 0)),
                   pl.BlockSpec((1, tm, d), lambda bi, i: (bi, i, 0))],
        out_shape=[jax.ShapeDtypeStruct((b, l, d), F32), jax.ShapeDtypeStruct((b, l, d), F32)],
        compiler_params=_cp("parallel", "parallel"),
        name="hyena_inproj",
    )(x, x, x, sh, sc, w, conv_w, conv_b.reshape(1, 3 * d))


def _hy_filter_kernel(z_ref, w1_ref, b1_ref, w2_ref, b2_ref, w3_ref, b3_ref, fr_ref, w4_ref,
                      t_ref, dl_ref, hf_ref, hb_ref, inv_ref, *, d):
    i = pl.program_id(0)
    fr = fr_ref[...]
    hdn = jnp.sin(fr * (_dot3(z_ref[...], w1_ref[...]) + b1_ref[...]))
    hdn = jnp.sin(fr * (_dot3(hdn, w2_ref[...]) + b2_ref[...]))
    hdn = jnp.sin(fr * (_dot3(hdn, w3_ref[...]) + b3_ref[...]))
    h = _dot3(hdn, w4_ref[...])
    decay = jnp.exp(-t_ref[...] * dl_ref[...])
    hf = h[:, :d] * decay
    hb = h[:, d:] * decay
    row = lax.broadcasted_iota(jnp.int32, hb.shape, 0)
    hb = jnp.where((row == 0) & (i == 0), 0.0, hb)
    hf_ref[...] = hf
    hb_ref[...] = hb

    @pl.when(i == 0)
    def _():
        inv_ref[...] = jnp.zeros_like(inv_ref)

    inv_ref[...] += (jnp.sum(jnp.abs(hf), axis=0, keepdims=True)
                     + jnp.sum(jnp.abs(hb), axis=0, keepdims=True))

    @pl.when(i == pl.num_programs(0) - 1)
    def _():
        inv_ref[...] = 1.0 / inv_ref[...]


def _hy_filters(l, d, w1, b1, w2, b2, w3, b3, freq, w4):
    hh = w1.shape[1]
    bands = (HY_FILTER_EMB - 1) // 2
    t = np.linspace(0.0, 1.0, l, dtype=np.float32)[:, None]
    w = (2.0 * math.pi * np.arange(l, dtype=np.float32)[:, None] / l).astype(np.float32)
    fb = np.linspace(1e-4, bands - 1, bands, dtype=np.float32)[None, :]
    z = np.concatenate([t, np.cos(fb * w), -np.sin(fb * w)], axis=-1).astype(np.float32)
    zp = np.zeros((l, V7X_LANES), np.float32)
    zp[:, :HY_FILTER_EMB] = z
    w1p = jnp.zeros((V7X_LANES, hh), F32).at[:HY_FILTER_EMB].set(w1.astype(F32))
    min_decay = math.log(HY_DECAY_TARGET) / HY_SLOW_DECAY
    max_decay = math.log(HY_DECAY_TARGET) / HY_FAST_DECAY
    deltas = np.abs(np.linspace(min_decay, max_decay, d, dtype=np.float32))[None, :]
    tl = _pick(l, (512, 256, 128))
    const = lambda i: (0, 0)
    r2 = lambda a: a.astype(F32).reshape(1, -1)
    return pl.pallas_call(
        functools.partial(_hy_filter_kernel, d=d),
        grid=(l // tl,),
        in_specs=[pl.BlockSpec((tl, V7X_LANES), lambda i: (i, 0)),
                  pl.BlockSpec((V7X_LANES, hh), const), pl.BlockSpec((1, hh), const),
                  pl.BlockSpec((hh, hh), const), pl.BlockSpec((1, hh), const),
                  pl.BlockSpec((hh, hh), const), pl.BlockSpec((1, hh), const),
                  pl.BlockSpec((1, hh), const), pl.BlockSpec((hh, 2 * d), const),
                  pl.BlockSpec((tl, 1), lambda i: (i, 0)), pl.BlockSpec((1, d), const)],
        out_specs=[pl.BlockSpec((tl, d), lambda i: (i, 0)), pl.BlockSpec((tl, d), lambda i: (i, 0)),
                   pl.BlockSpec((1, d), const)],
        out_shape=[jax.ShapeDtypeStruct((l, d), F32), jax.ShapeDtypeStruct((l, d), F32),
                   jax.ShapeDtypeStruct((1, d), F32)],
        compiler_params=_cp("arbitrary"),
        name="hyena_filter",
    )(jnp.asarray(zp), w1p, r2(b1), w2.astype(F32), r2(b2), w3.astype(F32), r2(b3), r2(freq),
      w4.astype(F32), jnp.asarray(t), jnp.asarray(deltas))


def _hyena_tables(l):
    r = l // DFT_INNER
    n = 2 * l
    k1 = np.arange(2 * r)[None, :, None]
    n1 = np.arange(r)[None, None, :]
    n2 = np.arange(DFT_INNER)[:, None, None]
    th = 2.0 * np.pi * ((k1 * (DFT_INNER * n1 + n2)) % n) / n
    g1 = np.concatenate([np.cos(th), -np.sin(th)], 1)
    k2 = np.arange(DFT_INNER)
    th3 = 2.0 * np.pi * np.outer(k2, k2) / DFT_INNER
    c, s = np.cos(th3), np.sin(th3)
    m3f = np.block([[c, s], [-s, c]])
    m3i = np.block([[c, -s], [s, c]])
    thi = np.transpose(th, (0, 2, 1))
    hinv = np.concatenate([np.cos(thi), -np.sin(thi)], 2) / n
    as_bf = lambda a: jnp.asarray(a, dtype=F32).astype(BF16)
    return as_bf(g1), as_bf(m3f), as_bf(m3i), as_bf(hinv)


def _hy1_kernel(v_ref, g_ref, o_ref, *, r2):
    a = _dot(g_ref[0], v_ref[0].astype(BF16))
    o_ref[0, 0, 0] = a[:r2]
    o_ref[0, 1, 0] = a[r2:]


def _hy_stage1(v, g1):
    nb, l, d = v.shape
    r = l // DFT_INNER
    return pl.pallas_call(
        functools.partial(_hy1_kernel, r2=2 * r),
        grid=(nb, DFT_INNER),
        in_specs=[pl.BlockSpec((1, r, d), lambda bi, j: (bi, 0, j)),
                  pl.BlockSpec((1, 4 * r, r), lambda bi, j: (j, 0, 0))],
        out_specs=pl.BlockSpec((1, 2, 1, 2 * r, d), lambda bi, j: (bi, 0, j, 0, 0)),
        out_shape=jax.ShapeDtypeStruct((nb, 2, DFT_INNER, 2 * r, d), F32),
        compiler_params=_cp("parallel", "parallel"),
        name="hyena_fft_stage1",
    )(v.reshape(nb, r, DFT_INNER * d), g1)


def _hy2_kernel(a_ref, fh_ref, fb_ref, inv_ref, m3f_ref, m3i_ref, o_ref, kr_ref, ki_ref):
    m3f = m3f_ref[...]
    half = DFT_INNER

    @pl.when(pl.program_id(1) == 0)
    def _():
        fh = _dot(m3f, fh_ref[0].astype(BF16))
        fb = _dot(m3f, fb_ref[0].astype(BF16))
        inv = inv_ref[...]
        kr_ref[...] = (fh[:half] + fb[:half]) * inv
        ki_ref[...] = (fh[half:] - fb[half:]) * inv

    vv = _dot(m3f, a_ref[0].astype(BF16))
    vr, vi = vv[:half], vv[half:]
    kr, ki = kr_ref[...], ki_ref[...]
    y = jnp.concatenate([vr * kr - vi * ki, vr * ki + vi * kr], axis=0).astype(BF16)
    o_ref[0] = _dot(m3i_ref[...], y)


def _hy_stage2(a, af, inv_tiled, m3f, m3i, tc):
    b = a.shape[0]
    cols = a.shape[2]
    p = 2 * DFT_INNER
    return pl.pallas_call(
        _hy2_kernel,
        grid=(cols // tc, b),
        in_specs=[pl.BlockSpec((1, p, tc), lambda j, bi: (bi, 0, j)),
                  pl.BlockSpec((1, p, tc), lambda j, bi: (0, 0, j)),
                  pl.BlockSpec((1, p, tc), lambda j, bi: (1, 0, j)),
                  pl.BlockSpec((1, tc), lambda j, bi: (0, 0)),
                  pl.BlockSpec((p, p), lambda j, bi: (0, 0)),
                  pl.BlockSpec((p, p), lambda j, bi: (0, 0))],
        out_specs=pl.BlockSpec((1, p, tc), lambda j, bi: (bi, 0, j)),
        out_shape=jax.ShapeDtypeStruct((b, p, cols), F32),
        scratch_shapes=[pltpu.VMEM((DFT_INNER, tc), F32), pltpu.VMEM((DFT_INNER, tc), F32)],
        compiler_params=_cp("parallel", "arbitrary"),
        name="hyena_fft_stage2",
    )(a, af, af, inv_tiled, m3f, m3i)


def _hy3_kernel(b_ref, hinv_ref, vx_ref, x0_ref, d_ref, o_ref):
    bb = jnp.concatenate([b_ref[0, 0, 0], b_ref[0, 1, 0]], axis=0).astype(BF16)
    y = _dot(hinv_ref[0], bb)
    o_ref[0] = ((y + vx_ref[0] * d_ref[...]) * x0_ref[0]).astype(o_ref.dtype)


def _hy_stage3(bc, hinv, vx, x0, d_skip):
    b, l, d = vx.shape
    r = l // DFT_INNER
    col = lambda bi, j: (bi, 0, j)
    return pl.pallas_call(
        _hy3_kernel,
        grid=(b, DFT_INNER),
        in_specs=[pl.BlockSpec((1, 2, 1, 2 * r, d), lambda bi, j: (bi, 0, j, 0, 0)),
                  pl.BlockSpec((1, r, 4 * r), lambda bi, j: (j, 0, 0)),
                  pl.BlockSpec((1, r, d), col), pl.BlockSpec((1, r, d), col),
                  pl.BlockSpec((1, d), lambda bi, j: (0, 0))],
        out_specs=pl.BlockSpec((1, r, d), col),
        out_shape=jax.ShapeDtypeStruct((b, r, DFT_INNER * d), BF16),
        compiler_params=_cp("parallel", "parallel"),
        name="hyena_fft_stage3",
    )(bc.reshape(b, 2, DFT_INNER, 2 * r, d), hinv, vx.reshape(b, r, DFT_INNER * d),
      x0.reshape(b, r, DFT_INNER * d), d_skip.astype(F32).reshape(1, d)).reshape(b, l, d)


def _hyena_long_conv(vx, x0, hf, hb, inv_norm, d_skip, tables):
    b, l, d = vx.shape
    r = l // DFT_INNER
    g1, m3f, m3i, hinv = tables
    cols = 2 * r * d
    tc = _pick(cols, (8192, 4096, 2048, 1024))
    a = _hy_stage1(vx, g1).reshape(b, 2 * DFT_INNER, cols)
    af = _hy_stage1(jnp.stack([hf, hb]), g1).reshape(2, 2 * DFT_INNER, cols)
    inv_tiled = jnp.tile(inv_norm, (1, tc // d))
    bc = _hy_stage2(a, af, inv_tiled, m3f, m3i, tc)
    return _hy_stage3(bc, hinv, vx, x0, d_skip)


def _rope_tables(l):
    rows = l // GRID_W
    axis_dim = DA_HEAD_DIM // 2
    pos_r = jnp.repeat(jnp.arange(rows, dtype=F32), GRID_W)
    pos_c = jnp.tile(jnp.arange(GRID_W, dtype=F32), rows)
    inv = ROPE_THETA ** (-jnp.arange(0, axis_dim, 2, dtype=F32) / axis_dim)
    ar, ac = pos_r[:, None] * inv, pos_c[:, None] * inv
    n = ar.shape[1]
    zeros = jnp.zeros((l, n), F32)
    cos64 = jnp.concatenate([jnp.cos(ar), jnp.cos(ar), jnp.cos(ac), jnp.cos(ac)], -1)
    sa64 = jnp.concatenate([-jnp.sin(ar), zeros, -jnp.sin(ac), zeros], -1)
    sb64 = jnp.concatenate([zeros, jnp.sin(ar), zeros, jnp.sin(ac)], -1)
    rep = V7X_LANES // DA_HEAD_DIM
    return jnp.tile(cos64, (1, rep)), jnp.tile(sa64, (1, rep)), jnp.tile(sb64, (1, rep))


def _lambda_init(layer_idx):
    return 0.8 - 0.6 * math.exp(-0.3 * layer_idx)


def kernel(x, c, ctx, c_ctx, mod_w, mod_b, ln1_g, ln1_b, ln2_g, ln2_b, ffn_w_in, ffn_conv_w, ffn_conv_b, ffn_w_out, da_w_in, da_w_out, da_lam_q1, da_lam_k1, da_lam_q2, da_lam_k2, da_subln_g, hy_w_in, hy_conv_w, hy_conv_b, hy_f_w1, hy_f_b1, hy_f_w2, hy_f_b2, hy_f_w3, hy_f_b3, hy_f_freq, hy_f_w4, hy_d, hy_w_out):
    b, l, d = x.shape
    depth = mod_w.shape[0]
    assert depth == 2 and l % (GRID_W * 8) == 0 and d % (2 * V7X_LANES) == 0
    da_w = d // 2
    alpha = (2 * depth) ** 0.25

    rows = -(-(b + 1) // 8) * 8
    cv = jnp.zeros((rows, d), F32).at[:b].set(c.astype(F32)).at[b].set(c_ctx.astype(F32))
    mod = _mod_params(cv, mod_w.astype(F32), mod_b.astype(F32))

    def mods(i):
        m = mod[i].reshape(rows, N_MOD, d)
        return [m[:b, k][:, None, :] for k in range(N_MOD)], [m[b:b + 1, k][:, None, :] for k in range(N_MOD)]

    (sh1, sc1, g1, sh2, sc2, g2), (csh1, csc1, _, _, _, _) = mods(0)
    w_in = da_w_in[0].astype(BF16)
    cos, sa, sb = _rope_tables(l)
    q, k, v, f = _inproj(x, sh1, sc1, w_in, cos, sa, sb, da_w)
    kvc = _lnmm(ctx, csh1, csc1, w_in[:, da_w:3 * da_w])
    lam_rows = jnp.zeros((8, V7X_LANES), F32)
    for r_i, p in enumerate((da_lam_q1[0], da_lam_k1[0], da_lam_q2[0], da_lam_k2[0])):
        lam_rows = lam_rows.at[r_i, :DA_HEAD_DIM].set(p.astype(F32))
    o = _diff_attention(q, k, v, kvc, lam_rows, da_subln_g[0].astype(F32).reshape(1, -1), _lambda_init(0))
    fm = _fourier_mix(f, _fourier_tables(l, d - da_w))
    w_out = da_w_out[0].astype(BF16)
    x = _proj_res_ln([o, fm], [w_out[:da_w], w_out[da_w:]], x, g1, ln1_g[0], ln1_b[0], alpha, "l0_outproj")
    x = _conv_ffn(x, sh2, sc2, g2, ffn_w_in[0].astype(BF16), ffn_conv_w[0].astype(F32),
                  ffn_conv_b[0].astype(F32), ffn_w_out[0].astype(BF16), ln2_g[0], ln2_b[0], alpha, "l0_ffn")

    (sh1, sc1, g1, sh2, sc2, g2), _ = mods(1)
    x0, vx = _hy_inproj(x, sh1, sc1, hy_w_in[0].astype(BF16), hy_conv_w[0].astype(F32),
                        hy_conv_b[0].astype(F32))
    hf, hb, inv_norm = _hy_filters(l, d, hy_f_w1[0], hy_f_b1[0], hy_f_w2[0], hy_f_b2[0],
                                   hy_f_w3[0], hy_f_b3[0], hy_f_freq[0], hy_f_w4[0])
    z = _hyena_long_conv(vx, x0, hf, hb, inv_norm, hy_d[0], _hyena_tables(l))
    x = _proj_res_ln([z], [hy_w_out[0].astype(BF16)], x, g1, ln1_g[1], ln1_b[1], alpha, "l1_outproj")
    x = _conv_ffn(x, sh2, sc2, g2, ffn_w_in[1].astype(BF16), ffn_conv_w[1].astype(F32),
                  ffn_conv_b[1].astype(F32), ffn_w_out[1].astype(BF16), ln2_g[1], ln2_b[1], alpha, "l1_ffn")
    return x
```

```python
import functools
import math

import numpy as np
import jax
import jax.numpy as jnp
from jax import lax
from jax.experimental import pallas as pl
from jax.experimental.pallas import tpu as pltpu

F32 = jnp.float32
BF16 = jnp.bfloat16

GRID_W = 64
DA_HEAD_DIM = 64
ROPE_THETA = 10000.0
N_MOD = 6
LN_EPS = 1e-5
FN_GROUPS = 4
HY_FILTER_EMB = 33
HY_DECAY_TARGET = 1e-2
HY_FAST_DECAY = 0.3
HY_SLOW_DECAY = 1.5
DFT_INNER = GRID_W

V7X_LANES = 128
V7X_VMEM_LIMIT_BYTES = 56 * 1024 * 1024
HALO = 8
HEAD_W = 2 * DA_HEAD_DIM
ONES_ROWS = 16
ATT_TK = 512


def _cp(*sem):
    return pltpu.CompilerParams(dimension_semantics=sem, vmem_limit_bytes=V7X_VMEM_LIMIT_BYTES)


def _dot(a, b):
    return jnp.dot(a, b, preferred_element_type=F32)


def _split(a):
    hi = a.astype(BF16)
    lo = (a - hi.astype(F32)).astype(BF16)
    return hi, lo


def _dot3(a, b):
    ah, al = _split(a)
    bh, bl = _split(b)
    return _dot(ah, bh) + _dot(al, bh) + _dot(ah, bl)


def _ln(x):
    mu = jnp.mean(x, axis=-1, keepdims=True)
    xc = x - mu
    var = jnp.mean(xc * xc, axis=-1, keepdims=True)
    return xc * lax.rsqrt(var + LN_EPS)


def _pick(n, prefs):
    for p in prefs:
        if n % p == 0:
            return p
    return n


def _mod_kernel(cv_ref, w_ref, b_ref, o_ref):
    cv = cv_ref[...]
    s = cv * jax.nn.sigmoid(cv)
    o_ref[0] = _dot3(s, w_ref[0]) + b_ref[0]


def _mod_params(cv, mod_w, mod_b):
    depth, d, n = mod_w.shape
    rows = cv.shape[0]
    tn = _pick(n, (1536, 1024, 512, 256, 128))
    return pl.pallas_call(
        _mod_kernel,
        grid=(depth, n // tn),
        in_specs=[pl.BlockSpec((rows, d), lambda i, j: (0, 0)),
                  pl.BlockSpec((1, d, tn), lambda i, j: (i, 0, j)),
                  pl.BlockSpec((1, 1, tn), lambda i, j: (i, 0, j))],
        out_specs=pl.BlockSpec((1, rows, tn), lambda i, j: (i, 0, j)),
        out_shape=jax.ShapeDtypeStruct((depth, rows, n), F32),
        compiler_params=_cp("parallel", "parallel"),
        name="mod_params",
    )(cv, mod_w, mod_b.reshape(depth, 1, n))


def _rope_block(blk, cos, sin_a, sin_b):
    return (blk * cos + pltpu.roll(blk, V7X_LANES - 16, 1) * sin_a + pltpu.roll(blk, 16, 1) * sin_b)


def _store_vT(v, vT_ref, head):
    vT_ref[0, head, 0, 0:HEAD_W, :] = v.T.astype(BF16)
    vT_ref[0, head, 0, HEAD_W:, :] = jnp.ones((ONES_ROWS, v.shape[0]), BF16)


def _inproj_kernel(x_ref, sh_ref, sc_ref, w_ref, cos_ref, sa_ref, sb_ref,
                   qT_ref, k_ref, vT_ref, f_ref, *, da_w, q_scale):
    h = (_ln(x_ref[0]) * (1.0 + sc_ref[0]) + sh_ref[0]).astype(BF16)
    p = _dot(h, w_ref[...])
    cos, sa, sb = cos_ref[...], sa_ref[...], sb_ref[...]
    for j in range(da_w // HEAD_W):
        lo, hi = j * HEAD_W, (j + 1) * HEAD_W
        qT_ref[0, lo:hi, :] = (_rope_block(p[:, lo:hi], cos, sa, sb) * q_scale).T.astype(BF16)
        k_ref[0, :, lo:hi] = _rope_block(p[:, da_w + lo:da_w + hi], cos, sa, sb).astype(BF16)
        _store_vT(p[:, 2 * da_w + lo:2 * da_w + hi], vT_ref, j)
    f_ref[0] = p[:, 3 * da_w:].astype(BF16)


def _inproj(x, sh, sc, w, cos, sa, sb, da_w):
    b, l, d = x.shape
    n = w.shape[1]
    fn_w = n - 3 * da_w
    heads = da_w // HEAD_W
    tm = ATT_TK
    row = lambda bi, i: (bi, i, 0)
    vec = lambda bi, i: (bi, 0, 0)
    tab = lambda bi, i: (i, 0)
    return pl.pallas_call(
        functools.partial(_inproj_kernel, da_w=da_w, q_scale=DA_HEAD_DIM ** -0.5 * math.log2(math.e)),
        grid=(b, l // tm),
        in_specs=[pl.BlockSpec((1, tm, d), row),
                  pl.BlockSpec((1, 1, d), vec), pl.BlockSpec((1, 1, d), vec),
                  pl.BlockSpec((d, n), lambda bi, i: (0, 0)),
                  pl.BlockSpec((tm, V7X_LANES), tab), pl.BlockSpec((tm, V7X_LANES), tab),
                  pl.BlockSpec((tm, V7X_LANES), tab)],
        out_specs=[pl.BlockSpec((1, da_w, tm), lambda bi, i: (bi, 0, i)),
                   pl.BlockSpec((1, tm, da_w), row),
                   pl.BlockSpec((1, heads, 1, HEAD_W + ONES_ROWS, tm), lambda bi, i: (bi, 0, i, 0, 0)),
                   pl.BlockSpec((1, tm, fn_w), row)],
        out_shape=[jax.ShapeDtypeStruct((b, da_w, l), BF16), jax.ShapeDtypeStruct((b, l, da_w), BF16),
                   jax.ShapeDtypeStruct((b, heads, l // tm, HEAD_W + ONES_ROWS, tm), BF16),
                   jax.ShapeDtypeStruct((b, l, fn_w), BF16)],
        compiler_params=_cp("parallel", "parallel"),
        name="l0_inproj",
    )(x, sh, sc, w, cos, sa, sb)


def _ctx_kv_kernel(x_ref, sh_ref, sc_ref, w_ref, k_ref, vT_ref, *, da_w):
    h = (_ln(x_ref[0]) * (1.0 + sc_ref[0]) + sh_ref[0]).astype(BF16)
    p = _dot(h, w_ref[...])
    k_ref[0] = p[:, :da_w].astype(BF16)
    for j in range(da_w // HEAD_W):
        _store_vT(p[:, da_w + j * HEAD_W:da_w + (j + 1) * HEAD_W], vT_ref, j)


def _ctx_kv(ctx, sh, sc, w, da_w):
    b, lc, d = ctx.shape
    heads = da_w // HEAD_W
    return pl.pallas_call(
        functools.partial(_ctx_kv_kernel, da_w=da_w),
        grid=(b,),
        in_specs=[pl.BlockSpec((1, lc, d), lambda bi: (bi, 0, 0)),
                  pl.BlockSpec((1, 1, d), lambda bi: (0, 0, 0)),
                  pl.BlockSpec((1, 1, d), lambda bi: (0, 0, 0)),
                  pl.BlockSpec((d, 2 * da_w), lambda bi: (0, 0))],
        out_specs=[pl.BlockSpec((1, lc, da_w), lambda bi: (bi, 0, 0)),
                   pl.BlockSpec((1, heads, 1, HEAD_W + ONES_ROWS, lc), lambda bi: (bi, 0, 0, 0, 0))],
        out_shape=[jax.ShapeDtypeStruct((b, lc, da_w), BF16),
                   jax.ShapeDtypeStruct((b, heads, 1, HEAD_W + ONES_ROWS, lc), BF16)],
        compiler_params=_cp("parallel"),
        name="ctx_kv_proj",
    )(ctx, sh, sc, w)


def _attn_kernel(qT_ref, kc_ref, vcT_ref, k_ref, vT_ref, lam_ref, g_ref, o_ref,
                 m1_ref, a1_ref, m2_ref, a2_ref, sa1_ref, sa2_ref, sb1_ref, sb2_ref, *, lam_init):
    qT = qT_ref[0]
    row = lax.broadcasted_iota(jnp.int32, qT.shape, 0)
    zero = jnp.zeros_like(qT)
    q1 = jnp.where(row < DA_HEAD_DIM, qT, zero)
    q2 = jnp.where(row >= DA_HEAD_DIM, qT, zero)
    nk = vT_ref.shape[2]

    m1_ref[...] = jnp.full_like(m1_ref, -jnp.inf)
    m2_ref[...] = jnp.full_like(m2_ref, -jnp.inf)
    a1_ref[...] = jnp.zeros_like(a1_ref)
    a2_ref[...] = jnp.zeros_like(a2_ref)

    def softmax_pv(s, vb, m_ref, a_ref):
        m_old = m_ref[...]
        m_new = jnp.maximum(m_old, jnp.max(s, axis=0, keepdims=True))
        p = jnp.exp2(s - m_new).astype(BF16)
        a_ref[...] = a_ref[...] * jnp.exp2(m_old - m_new) + _dot(vb, p)
        m_ref[...] = m_new

    def scores(t, s1_ref, s2_ref):
        off = pl.multiple_of(t * ATT_TK, ATT_TK)
        kb = k_ref[0, pl.ds(off, ATT_TK), :]
        s1_ref[...] = _dot(kb, q1)
        s2_ref[...] = _dot(kb, q2)

    def consume(t, s1_ref, s2_ref):
        vb = vT_ref[0, 0, t]
        softmax_pv(s1_ref[...], vb, m1_ref, a1_ref)
        softmax_pv(s2_ref[...], vb, m2_ref, a2_ref)

    kc, vc = kc_ref[0], vcT_ref[0, 0, 0]
    softmax_pv(_dot(kc, q1), vc, m1_ref, a1_ref)
    softmax_pv(_dot(kc, q2), vc, m2_ref, a2_ref)

    scores(0, sa1_ref, sa2_ref)

    def body(j, carry):
        t = 2 * j
        scores(t + 1, sb1_ref, sb2_ref)
        consume(t, sa1_ref, sa2_ref)
        scores(jnp.minimum(t + 2, nk - 1), sa1_ref, sa2_ref)
        consume(t + 1, sb1_ref, sb2_ref)
        return carry

    lax.fori_loop(0, nk // 2, body, 0)

    lp = lam_ref[...]
    lam = (jnp.exp(jnp.sum(lp[0:1] * lp[1:2], axis=-1, keepdims=True))
           - jnp.exp(jnp.sum(lp[2:3] * lp[3:4], axis=-1, keepdims=True)) + lam_init)
    a1, a2 = a1_ref[...], a2_ref[...]
    o = (a1[:HEAD_W] * (1.0 / a1[HEAD_W:HEAD_W + 1])
         - lam * (a2[:HEAD_W] * (1.0 / a2[HEAD_W:HEAD_W + 1])))
    ms = jnp.mean(o * o, axis=0, keepdims=True)
    on = (o * lax.rsqrt(ms + LN_EPS)).T
    o_ref[0] = (on * g_ref[...] * (1.0 - lam_init)).astype(o_ref.dtype)


def _diff_attention(qT, k, vT, kc, vcT, lam_rows, subln_g, lam_init):
    b, da_w, l = qT.shape
    lc = kc.shape[1]
    heads = da_w // HEAD_W
    nk = vT.shape[2]
    av = HEAD_W + ONES_ROWS
    tq = _pick(l, (512, 256, 128))
    kmap = lambda bi, h, i: (bi, 0, h)
    vmap = lambda bi, h, i: (bi, h, 0, 0, 0)
    return pl.pallas_call(
        functools.partial(_attn_kernel, lam_init=lam_init),
        grid=(b, heads, l // tq),
        in_specs=[pl.BlockSpec((1, HEAD_W, tq), lambda bi, h, i: (bi, h, i)),
                  pl.BlockSpec((1, lc, HEAD_W), kmap),
                  pl.BlockSpec((1, 1, 1, av, lc), vmap),
                  pl.BlockSpec((1, l, HEAD_W), kmap),
                  pl.BlockSpec((1, 1, nk, av, ATT_TK), vmap),
                  pl.BlockSpec((8, V7X_LANES), lambda bi, h, i: (0, 0)),
                  pl.BlockSpec((1, HEAD_W), lambda bi, h, i: (0, 0))],
        out_specs=pl.BlockSpec((1, tq, HEAD_W), lambda bi, h, i: (bi, i, h)),
        out_shape=jax.ShapeDtypeStruct((b, l, da_w), BF16),
        scratch_shapes=[pltpu.VMEM((1, tq), F32), pltpu.VMEM((av, tq), F32),
                        pltpu.VMEM((1, tq), F32), pltpu.VMEM((av, tq), F32)]
                       + [pltpu.VMEM((ATT_TK, tq), F32)] * 4,
        compiler_params=_cp("parallel", "parallel", "parallel"),
        name="diff_attention",
    )(qT, kc, vcT, k, vT, lam_rows, subln_g)


def _fourier_tables(l, fn_w):
    r = l // DFT_INNER
    gw = fn_w // FN_GROUPS
    cc = np.arange(gw)
    th = 2.0 * np.pi * np.outer(cc, cc) / gw
    eye = np.eye(FN_GROUPS)
    wc = np.kron(eye, np.cos(th) / np.sqrt(gw))
    ws = np.kron(eye, -np.sin(th) / np.sqrt(gw))
    k1 = np.arange(r)[None, :, None]
    n1 = np.arange(r)[None, None, :]
    n2 = np.arange(DFT_INNER)[:, None, None]
    th1 = 2.0 * np.pi * ((k1 * (DFT_INNER * n1 + n2)) % l) / l
    gr, gi = np.cos(th1), -np.sin(th1)
    g = np.concatenate([np.concatenate([gr, -gi], 2), np.concatenate([gi, gr], 2)], 1)
    k2 = np.arange(DFT_INNER)
    th3 = 2.0 * np.pi * np.outer(k2, k2) / DFT_INNER
    m3 = np.concatenate([np.cos(th3), np.sin(th3)], 1) / np.sqrt(l)
    as_bf = lambda a: jnp.asarray(a, dtype=F32).astype(BF16)
    return as_bf(wc), as_bf(ws), as_bf(g), as_bf(m3)


def _fm1_kernel(x_ref, wc_ref, ws_ref, g_ref, o_ref, *, r):
    x = x_ref[0]
    zs = jnp.concatenate([_dot(x, wc_ref[...]), _dot(x, ws_ref[...])], axis=0).astype(BF16)
    a = _dot(g_ref[0], zs)
    o_ref[0, 0, 0] = a[:r]
    o_ref[0, 1, 0] = a[r:]


def _fm2_kernel(a_ref, m3_ref, o_ref):
    o_ref[0] = _dot(m3_ref[...], a_ref[0].astype(BF16)).astype(o_ref.dtype)


def _fourier_mix(f, tables):
    b, l, fn_w = f.shape
    r = l // DFT_INNER
    wc, ws, g, m3 = tables
    a = pl.pallas_call(
        functools.partial(_fm1_kernel, r=r),
        grid=(b, DFT_INNER),
        in_specs=[pl.BlockSpec((1, r, fn_w), lambda bi, j: (bi, 0, j)),
                  pl.BlockSpec((fn_w, fn_w), lambda bi, j: (0, 0)),
                  pl.BlockSpec((fn_w, fn_w), lambda bi, j: (0, 0)),
                  pl.BlockSpec((1, 2 * r, 2 * r), lambda bi, j: (j, 0, 0))],
        out_specs=pl.BlockSpec((1, 2, 1, r, fn_w), lambda bi, j: (bi, 0, j, 0, 0)),
        out_shape=jax.ShapeDtypeStruct((b, 2, DFT_INNER, r, fn_w), F32),
        compiler_params=_cp("parallel", "parallel"),
        name="fourier_stage1",
    )(f.reshape(b, r, DFT_INNER * fn_w), wc, ws, g)
    cols = r * fn_w
    tc = _pick(cols, (8192, 4096, 2048, 1024, 512))
    out = pl.pallas_call(
        _fm2_kernel,
        grid=(b, cols // tc),
        in_specs=[pl.BlockSpec((1, 2 * DFT_INNER, tc), lambda bi, j: (bi, 0, j)),
                  pl.BlockSpec((DFT_INNER, 2 * DFT_INNER), lambda bi, j: (0, 0))],
        out_specs=pl.BlockSpec((1, DFT_INNER, tc), lambda bi, j: (bi, 0, j)),
        out_shape=jax.ShapeDtypeStruct((b, DFT_INNER, cols), BF16),
        compiler_params=_cp("parallel", "parallel"),
        name="fourier_stage2",
    )(a.reshape(b, 2 * DFT_INNER, cols), m3)
    return out.reshape(b, l, fn_w)


def _proj_res_ln_kernel(*refs, n_in, alpha):
    ins = refs[:n_in]
    ws = refs[n_in:2 * n_in]
    x_ref, gate_ref, g_ref, b_ref, o_ref = refs[2 * n_in:]
    y = _dot(ins[0][0], ws[0][...])
    for a_ref, w_ref in zip(ins[1:], ws[1:]):
        y = y + _dot(a_ref[0], w_ref[...])
    z = alpha * x_ref[0] + gate_ref[0] * y
    o_ref[0] = _ln(z) * g_ref[...] + b_ref[...]


def _proj_res_ln(acts, weights, x, gate, ln_g, ln_b, alpha, name):
    b, l, d = x.shape
    tm = _pick(l, (512, 256, 128))
    row = lambda bi, i: (bi, i, 0)
    const = lambda bi, i: (0, 0)
    in_specs = ([pl.BlockSpec((1, tm, a.shape[2]), row) for a in acts]
                + [pl.BlockSpec(w.shape, const) for w in weights]
                + [pl.BlockSpec((1, tm, d), row), pl.BlockSpec((1, 1, d), lambda bi, i: (bi, 0, 0)),
                   pl.BlockSpec((1, d), const), pl.BlockSpec((1, d), const)])
    return pl.pallas_call(
        functools.partial(_proj_res_ln_kernel, n_in=len(acts), alpha=alpha),
        grid=(b, l // tm),
        in_specs=in_specs,
        out_specs=pl.BlockSpec((1, tm, d), row),
        out_shape=jax.ShapeDtypeStruct((b, l, d), F32),
        compiler_params=_cp("parallel", "parallel"),
        name=name,
    )(*acts, *weights, x, gate, ln_g.reshape(1, d), ln_b.reshape(1, d))


def _halo_ln(xp_ref, x_ref, xn_ref, sh_ref, sc_ref, first, last):
    mod = lambda v: _ln(v) * (1.0 + sc_ref[0]) + sh_ref[0]
    hp = mod(xp_ref[0]) * jnp.where(first, 0.0, 1.0)
    hn = mod(xn_ref[0]) * jnp.where(last, 0.0, 1.0)
    return jnp.concatenate([hp, mod(x_ref[0]), hn], axis=0).astype(BF16)


def _dwconv3(pre, cw, cb):
    n_rows = pre.shape[0]
    up = pltpu.roll(pre, 1, 0)
    dn = pltpu.roll(pre, n_rows - 1, 0)
    u = up * cw[0:1] + pre * cw[1:2] + dn * cw[2:3] + cb
    return u[HALO:n_rows - HALO]


def _halo_specs(tm, d, l):
    nb = tm // HALO
    last_blk = l // HALO - 1
    return [pl.BlockSpec((1, HALO, d), lambda bi, i, *_: (bi, jnp.maximum(i * nb - 1, 0), 0)),
            pl.BlockSpec((1, tm, d), lambda bi, i, *_: (bi, i, 0)),
            pl.BlockSpec((1, HALO, d), lambda bi, i, *_: (bi, jnp.minimum((i + 1) * nb, last_blk), 0))]


def _erf(x):
    return lax.erf(x)


def _ffn_kernel(xp_ref, x_ref, xn_ref, sh_ref, sc_ref, gate_ref,
                wa_ref, wg_ref, cwa_ref, cwg_ref, cba_ref, cbg_ref, wo_ref, g_ref, b_ref,
                o_ref, h_ref, acc_ref, *, alpha):
    i, j = pl.program_id(1), pl.program_id(2)

    @pl.when(j == 0)
    def _():
        h_ref[...] = _halo_ln(xp_ref, x_ref, xn_ref, sh_ref, sc_ref,
                              i == 0, i == pl.num_programs(1) - 1)
        acc_ref[...] = jnp.zeros_like(acc_ref)

    h = h_ref[...]
    a = _dwconv3(_dot(h, wa_ref[...]), cwa_ref[...], cba_ref[...])
    g = _dwconv3(_dot(h, wg_ref[...]), cwg_ref[...], cbg_ref[...])
    z = (0.5 * a * (1.0 + _erf(a * (2.0 ** -0.5))) * g).astype(BF16)
    acc_ref[...] += _dot(z, wo_ref[...])

    @pl.when(j == pl.num_programs(2) - 1)
    def _():
        zz = alpha * x_ref[0] + gate_ref[0] * acc_ref[...]
        o_ref[0] = _ln(zz) * g_ref[...] + b_ref[...]


def _conv_ffn(x, sh, sc, gate, w_in, conv_w, conv_b, w_out, ln_g, ln_b, alpha, name):
    b, l, d = x.shape
    f = w_out.shape[0]
    tm = _pick(l, (1024, 512, 256, 128))
    tf = _pick(f, (256, 128))
    nf = f // tf
    vec = lambda bi, i, j: (bi, 0, 0)
    const = lambda bi, i, j: (0, 0)
    cb2 = conv_b.reshape(1, 2 * f)
    return pl.pallas_call(
        functools.partial(_ffn_kernel, alpha=alpha),
        grid=(b, l // tm, nf),
        in_specs=_halo_specs(tm, d, l) + [
            pl.BlockSpec((1, 1, d), vec), pl.BlockSpec((1, 1, d), vec), pl.BlockSpec((1, 1, d), vec),
            pl.BlockSpec((d, tf), lambda bi, i, j: (0, j)),
            pl.BlockSpec((d, tf), lambda bi, i, j: (0, nf + j)),
            pl.BlockSpec((3, tf), lambda bi, i, j: (0, j)),
            pl.BlockSpec((3, tf), lambda bi, i, j: (0, nf + j)),
            pl.BlockSpec((1, tf), lambda bi, i, j: (0, j)),
            pl.BlockSpec((1, tf), lambda bi, i, j: (0, nf + j)),
            pl.BlockSpec((tf, d), lambda bi, i, j: (j, 0)),
            pl.BlockSpec((1, d), const), pl.BlockSpec((1, d), const)],
        out_specs=pl.BlockSpec((1, tm, d), lambda bi, i, j: (bi, i, 0)),
        out_shape=jax.ShapeDtypeStruct((b, l, d), F32),
        scratch_shapes=[pltpu.VMEM((tm + 2 * HALO, d), BF16), pltpu.VMEM((tm, d), F32)],
        compiler_params=_cp("parallel", "parallel", "arbitrary"),
        name=name,
    )(x, x, x, sh, sc, gate, w_in, w_in, conv_w, conv_w, cb2, cb2, w_out,
      ln_g.reshape(1, d), ln_b.reshape(1, d))


def _hy_inproj_kernel(xp_ref, x_ref, xn_ref, sh_ref, sc_ref, w_ref, cw_ref, cb_ref,
                      x0_ref, vx_ref, *, d, tn):
    i = pl.program_id(1)
    h = _halo_ln(xp_ref, x_ref, xn_ref, sh_ref, sc_ref, i == 0, i == pl.num_programs(1) - 1)
    for c in range(d // tn):
        def part(k):
            lo = k * d + c * tn
            return _dwconv3(_dot(h, w_ref[:, lo:lo + tn]), cw_ref[:, lo:lo + tn], cb_ref[:, lo:lo + tn])
        x0_ref[0, :, c * tn:(c + 1) * tn] = part(0)
        vx_ref[0, :, c * tn:(c + 1) * tn] = part(2) * part(1)


def _hy_inproj(x, sh, sc, w, conv_w, conv_b):
    b, l, d = x.shape
    tm = _pick(l, (512, 256, 128))
    tn = _pick(d, (256, 128))
    vec = lambda bi, i: (bi, 0, 0)
    const = lambda bi, i: (0, 0)
    return pl.pallas_call(
        functools.partial(_hy_inproj_kernel, d=d, tn=tn),
        grid=(b, l // tm),
        in_specs=_halo_specs(tm, d, l) + [
            pl.BlockSpec((1, 1, d), vec), pl.BlockSpec((1, 1, d), vec),
            pl.BlockSpec((d, 3 * d), const), pl.BlockSpec((3, 3 * d), const),
            pl.BlockSpec((1, 3 * d), const)],
        out_specs=[pl.BlockSpec((1, tm, d), lambda bi, i: (bi, i, 0)),
                   pl.BlockSpec((1, tm, d), lambda bi, i: (bi, i, 0))],
        out_shape=[jax.ShapeDtypeStruct((b, l, d), F32), jax.ShapeDtypeStruct((b, l, d), F32)],
        compiler_params=_cp("parallel", "parallel"),
        name="hyena_inproj",
    )(x, x, x, sh, sc, w, conv_w, conv_b.reshape(1, 3 * d))


def _hy_filter_kernel(z_ref, w1_ref, b1_ref, w2_ref, b2_ref, w3_ref, b3_ref, fr_ref, w4_ref,
                      t_ref, dl_ref, hf_ref, hb_ref, inv_ref, *, d):
    i = pl.program_id(0)
    fr = fr_ref[...]
    hdn = jnp.sin(fr * (_dot3(z_ref[...], w1_ref[...]) + b1_ref[...]))
    hdn = jnp.sin(fr * (_dot3(hdn, w2_ref[...]) + b2_ref[...]))
    hdn = jnp.sin(fr * (_dot3(hdn, w3_ref[...]) + b3_ref[...]))
    h = _dot3(hdn, w4_ref[...])
    decay = jnp.exp(-t_ref[...] * dl_ref[...])
    hf = h[:, :d] * decay
    hb = h[:, d:] * decay
    row = lax.broadcasted_iota(jnp.int32, hb.shape, 0)
    hb = jnp.where((row == 0) & (i == 0), 0.0, hb)
    hf_ref[...] = hf
    hb_ref[...] = hb

    @pl.when(i == 0)
    def _():
        inv_ref[...] = jnp.zeros_like(inv_ref)

    inv_ref[...] += (jnp.sum(jnp.abs(hf), axis=0, keepdims=True)
                     + jnp.sum(jnp.abs(hb), axis=0, keepdims=True))

    @pl.when(i == pl.num_programs(0) - 1)
    def _():
        inv_ref[...] = 1.0 / inv_ref[...]


def _hy_filters(l, d, w1, b1, w2, b2, w3, b3, freq, w4):
    hh = w1.shape[1]
    bands = (HY_FILTER_EMB - 1) // 2
    t = np.linspace(0.0, 1.0, l, dtype=np.float32)[:, None]
    w = (2.0 * math.pi * np.arange(l, dtype=np.float32)[:, None] / l).astype(np.float32)
    fb = np.linspace(1e-4, bands - 1, bands, dtype=np.float32)[None, :]
    z = np.concatenate([t, np.cos(fb * w), -np.sin(fb * w)], axis=-1).astype(np.float32)
    zp = np.zeros((l, V7X_LANES), np.float32)
    zp[:, :HY_FILTER_EMB] = z
    w1p = jnp.zeros((V7X_LANES, hh), F32).at[:HY_FILTER_EMB].set(w1.astype(F32))
    min_decay = math.log(HY_DECAY_TARGET) / HY_SLOW_DECAY
    max_decay = math.log(HY_DECAY_TARGET) / HY_FAST_DECAY
    deltas = np.abs(np.linspace(min_decay, max_decay, d, dtype=np.float32))[None, :]
    tl = _pick(l, (512, 256, 128))
    const = lambda i: (0, 0)
    r2 = lambda a: a.astype(F32).reshape(1, -1)
    return pl.pallas_call(
        functools.partial(_hy_filter_kernel, d=d),
        grid=(l // tl,),
        in_specs=[pl.BlockSpec((tl, V7X_LANES), lambda i: (i, 0)),
                  pl.BlockSpec((V7X_LANES, hh), const), pl.BlockSpec((1, hh), const),
                  pl.BlockSpec((hh, hh), const), pl.BlockSpec((1, hh), const),
                  pl.BlockSpec((hh, hh), const), pl.BlockSpec((1, hh), const),
                  pl.BlockSpec((1, hh), const), pl.BlockSpec((hh, 2 * d), const),
                  pl.BlockSpec((tl, 1), lambda i: (i, 0)), pl.BlockSpec((1, d), const)],
        out_specs=[pl.BlockSpec((tl, d), lambda i: (i, 0)), pl.BlockSpec((tl, d), lambda i: (i, 0)),
                   pl.BlockSpec((1, d), const)],
        out_shape=[jax.ShapeDtypeStruct((l, d), F32), jax.ShapeDtypeStruct((l, d), F32),
                   jax.ShapeDtypeStruct((1, d), F32)],
        compiler_params=_cp("arbitrary"),
        name="hyena_filter",
    )(jnp.asarray(zp), w1p, r2(b1), w2.astype(F32), r2(b2), w3.astype(F32), r2(b3), r2(freq),
      w4.astype(F32), jnp.asarray(t), jnp.asarray(deltas))


def _hyena_tables(l):
    r = l // DFT_INNER
    n = 2 * l
    k1 = np.arange(2 * r)[None, :, None]
    n1 = np.arange(r)[None, None, :]
    n2 = np.arange(DFT_INNER)[:, None, None]
    th = 2.0 * np.pi * ((k1 * (DFT_INNER * n1 + n2)) % n) / n
    g1 = np.concatenate([np.cos(th), -np.sin(th)], 1)
    k2 = np.arange(DFT_INNER)
    th3 = 2.0 * np.pi * np.outer(k2, k2) / DFT_INNER
    c, s = np.cos(th3), np.sin(th3)
    m3f = np.block([[c, s], [-s, c]])
    m3i = np.block([[c, -s], [s, c]])
    thi = np.transpose(th, (0, 2, 1))
    hinv = np.concatenate([np.cos(thi), -np.sin(thi)], 2) / n
    as_bf = lambda a: jnp.asarray(a, dtype=F32).astype(BF16)
    return as_bf(g1), as_bf(m3f), as_bf(m3i), as_bf(hinv)


def _hy1_kernel(v_ref, g_ref, o_ref, *, r2):
    a = _dot(g_ref[0], v_ref[0].astype(BF16))
    o_ref[0, 0, 0] = a[:r2]
    o_ref[0, 1, 0] = a[r2:]


def _hy_stage1(v, g1):
    nb, l, d = v.shape
    r = l // DFT_INNER
    return pl.pallas_call(
        functools.partial(_hy1_kernel, r2=2 * r),
        grid=(nb, DFT_INNER),
        in_specs=[pl.BlockSpec((1, r, d), lambda bi, j: (bi, 0, j)),
                  pl.BlockSpec((1, 4 * r, r), lambda bi, j: (j, 0, 0))],
        out_specs=pl.BlockSpec((1, 2, 1, 2 * r, d), lambda bi, j: (bi, 0, j, 0, 0)),
        out_shape=jax.ShapeDtypeStruct((nb, 2, DFT_INNER, 2 * r, d), F32),
        compiler_params=_cp("parallel", "parallel"),
        name="hyena_fft_stage1",
    )(v.reshape(nb, r, DFT_INNER * d), g1)


def _hy2_kernel(a_ref, fh_ref, fb_ref, inv_ref, m3f_ref, m3i_ref, o_ref, kr_ref, ki_ref):
    m3f = m3f_ref[...]
    half = DFT_INNER

    @pl.when(pl.program_id(1) == 0)
    def _():
        fh = _dot(m3f, fh_ref[0].astype(BF16))
        fb = _dot(m3f, fb_ref[0].astype(BF16))
        inv = inv_ref[...]
        kr_ref[...] = (fh[:half] + fb[:half]) * inv
        ki_ref[...] = (fh[half:] - fb[half:]) * inv

    vv = _dot(m3f, a_ref[0].astype(BF16))
    vr, vi = vv[:half], vv[half:]
    kr, ki = kr_ref[...], ki_ref[...]
    y = jnp.concatenate([vr * kr - vi * ki, vr * ki + vi * kr], axis=0).astype(BF16)
    o_ref[0] = _dot(m3i_ref[...], y)


def _hy_stage2(a, af, inv_tiled, m3f, m3i, tc):
    b = a.shape[0]
    cols = a.shape[2]
    p = 2 * DFT_INNER
    return pl.pallas_call(
        _hy2_kernel,
        grid=(cols // tc, b),
        in_specs=[pl.BlockSpec((1, p, tc), lambda j, bi: (bi, 0, j)),
                  pl.BlockSpec((1, p, tc), lambda j, bi: (0, 0, j)),
                  pl.BlockSpec((1, p, tc), lambda j, bi: (1, 0, j)),
                  pl.BlockSpec((1, tc), lambda j, bi: (0, 0)),
                  pl.BlockSpec((p, p), lambda j, bi: (0, 0)),
                  pl.BlockSpec((p, p), lambda j, bi: (0, 0))],
        out_specs=pl.BlockSpec((1, p, tc), lambda j, bi: (bi, 0, j)),
        out_shape=jax.ShapeDtypeStruct((b, p, cols), F32),
        scratch_shapes=[pltpu.VMEM((DFT_INNER, tc), F32), pltpu.VMEM((DFT_INNER, tc), F32)],
        compiler_params=_cp("parallel", "arbitrary"),
        name="hyena_fft_stage2",
    )(a, af, af, inv_tiled, m3f, m3i)


def _hy3_kernel(b_ref, hinv_ref, vx_ref, x0_ref, d_ref, o_ref):
    bb = jnp.concatenate([b_ref[0, 0, 0], b_ref[0, 1, 0]], axis=0).astype(BF16)
    y = _dot(hinv_ref[0], bb)
    o_ref[0] = ((y + vx_ref[0] * d_ref[...]) * x0_ref[0]).astype(o_ref.dtype)


def _hy_stage3(bc, hinv, vx, x0, d_skip):
    b, l, d = vx.shape
    r = l // DFT_INNER
    col = lambda bi, j: (bi, 0, j)
    return pl.pallas_call(
        _hy3_kernel,
        grid=(b, DFT_INNER),
        in_specs=[pl.BlockSpec((1, 2, 1, 2 * r, d), lambda bi, j: (bi, 0, j, 0, 0)),
                  pl.BlockSpec((1, r, 4 * r), lambda bi, j: (j, 0, 0)),
                  pl.BlockSpec((1, r, d), col), pl.BlockSpec((1, r, d), col),
                  pl.BlockSpec((1, d), lambda bi, j: (0, 0))],
        out_specs=pl.BlockSpec((1, r, d), col),
        out_shape=jax.ShapeDtypeStruct((b, r, DFT_INNER * d), BF16),
        compiler_params=_cp("parallel", "parallel"),
        name="hyena_fft_stage3",
    )(bc.reshape(b, 2, DFT_INNER, 2 * r, d), hinv, vx.reshape(b, r, DFT_INNER * d),
      x0.reshape(b, r, DFT_INNER * d), d_skip.astype(F32).reshape(1, d)).reshape(b, l, d)


def _hyena_long_conv(vx, x0, hf, hb, inv_norm, d_skip, tables):
    b, l, d = vx.shape
    r = l // DFT_INNER
    g1, m3f, m3i, hinv = tables
    cols = 2 * r * d
    tc = _pick(cols, (8192, 4096, 2048, 1024))
    a = _hy_stage1(vx, g1).reshape(b, 2 * DFT_INNER, cols)
    af = _hy_stage1(jnp.stack([hf, hb]), g1).reshape(2, 2 * DFT_INNER, cols)
    inv_tiled = jnp.tile(inv_norm, (1, tc // d))
    bc = _hy_stage2(a, af, inv_tiled, m3f, m3i, tc)
    return _hy_stage3(bc, hinv, vx, x0, d_skip)


def _rope_tables(l):
    rows = l // GRID_W
    axis_dim = DA_HEAD_DIM // 2
    pos_r = jnp.repeat(jnp.arange(rows, dtype=F32), GRID_W)
    pos_c = jnp.tile(jnp.arange(GRID_W, dtype=F32), rows)
    inv = ROPE_THETA ** (-jnp.arange(0, axis_dim, 2, dtype=F32) / axis_dim)
    ar, ac = pos_r[:, None] * inv, pos_c[:, None] * inv
    n = ar.shape[1]
    zeros = jnp.zeros((l, n), F32)
    cos64 = jnp.concatenate([jnp.cos(ar), jnp.cos(ar), jnp.cos(ac), jnp.cos(ac)], -1)
    sa64 = jnp.concatenate([-jnp.sin(ar), zeros, -jnp.sin(ac), zeros], -1)
    sb64 = jnp.concatenate([zeros, jnp.sin(ar), zeros, jnp.sin(ac)], -1)
    rep = V7X_LANES // DA_HEAD_DIM
    return jnp.tile(cos64, (1, rep)), jnp.tile(sa64, (1, rep)), jnp.tile(sb64, (1, rep))


def _lambda_init(layer_idx):
    return 0.8 - 0.6 * math.exp(-0.3 * layer_idx)


def kernel(x, c, ctx, c_ctx, mod_w, mod_b, ln1_g, ln1_b, ln2_g, ln2_b, ffn_w_in, ffn_conv_w, ffn_conv_b, ffn_w_out, da_w_in, da_w_out, da_lam_q1, da_lam_k1, da_lam_q2, da_lam_k2, da_subln_g, hy_w_in, hy_conv_w, hy_conv_b, hy_f_w1, hy_f_b1, hy_f_w2, hy_f_b2, hy_f_w3, hy_f_b3, hy_f_freq, hy_f_w4, hy_d, hy_w_out):
    b, l, d = x.shape
    depth = mod_w.shape[0]
    assert depth == 2 and l % ATT_TK == 0 and l % (GRID_W * 8) == 0 and d % (2 * V7X_LANES) == 0
    da_w = d // 2
    alpha = (2 * depth) ** 0.25

    rows = -(-(b + 1) // 8) * 8
    cv = jnp.zeros((rows, d), F32).at[:b].set(c.astype(F32)).at[b].set(c_ctx.astype(F32))
    mod = _mod_params(cv, mod_w.astype(F32), mod_b.astype(F32))

    def mods(i):
        m = mod[i].reshape(rows, N_MOD, d)
        return [m[:b, k][:, None, :] for k in range(N_MOD)], [m[b:b + 1, k][:, None, :] for k in range(N_MOD)]

    (sh1, sc1, g1, sh2, sc2, g2), (csh1, csc1, _, _, _, _) = mods(0)
    w_in = da_w_in[0].astype(BF16)
    cos, sa, sb = _rope_tables(l)
    qT, k, vT, f = _inproj(x, sh1, sc1, w_in, cos, sa, sb, da_w)
    kc, vcT = _ctx_kv(ctx, csh1, csc1, w_in[:, da_w:3 * da_w], da_w)
    lam_rows = jnp.zeros((8, V7X_LANES), F32)
    for r_i, p in enumerate((da_lam_q1[0], da_lam_k1[0], da_lam_q2[0], da_lam_k2[0])):
        lam_rows = lam_rows.at[r_i, :DA_HEAD_DIM].set(p.astype(F32))
    o = _diff_attention(qT, k, vT, kc, vcT, lam_rows, da_subln_g[0].astype(F32).reshape(1, -1), _lambda_init(0))
    fm = _fourier_mix(f, _fourier_tables(l, d - da_w))
    w_out = da_w_out[0].astype(BF16)
    x = _proj_res_ln([o, fm], [w_out[:da_w], w_out[da_w:]], x, g1, ln1_g[0], ln1_b[0], alpha, "l0_outproj")
    x = _conv_ffn(x, sh2, sc2, g2, ffn_w_in[0].astype(BF16), ffn_conv_w[0].astype(F32),
                  ffn_conv_b[0].astype(F32), ffn_w_out[0].astype(BF16), ln2_g[0], ln2_b[0], alpha, "l0_ffn")

    (sh1, sc1, g1, sh2, sc2, g2), _ = mods(1)
    x0, vx = _hy_inproj(x, sh1, sc1, hy_w_in[0].astype(BF16), hy_conv_w[0].astype(F32),
                        hy_conv_b[0].astype(F32))
    hf, hb, inv_norm = _hy_filters(l, d, hy_f_w1[0], hy_f_b1[0], hy_f_w2[0], hy_f_b2[0],
                                   hy_f_w3[0], hy_f_b3[0], hy_f_freq[0], hy_f_w4[0])
    z = _hyena_long_conv(vx, x0, hf, hb, inv_norm, hy_d[0], _hyena_tables(l))
    x = _proj_res_ln([z], [hy_w_out[0].astype(BF16)], x, g1, ln1_g[1], ln1_b[1], alpha, "l1_outproj")
    x = _conv_ffn(x, sh2, sc2, g2, ffn_w_in[1].astype(BF16), ffn_conv_w[1].astype(F32),
                  ffn_conv_b[1].astype(F32), ffn_w_out[1].astype(BF16), ln2_g[1], ln2_b[1], alpha, "l1_ffn")
    return x
```

```python
import functools
import math

import numpy as np
import jax
import jax.numpy as jnp
from jax import lax
from jax.experimental import pallas as pl
from jax.experimental.pallas import tpu as pltpu

F32 = jnp.float32
BF16 = jnp.bfloat16

GRID_W = 64
DA_HEAD_DIM = 64
ROPE_THETA = 10000.0
N_MOD = 6
LN_EPS = 1e-5
FN_GROUPS = 4
HY_FILTER_EMB = 33
HY_DECAY_TARGET = 1e-2
HY_FAST_DECAY = 0.3
HY_SLOW_DECAY = 1.5
DFT_INNER = GRID_W

V7X_LANES = 128
V7X_VMEM_LIMIT_BYTES = 56 * 1024 * 1024
HALO = 8
HEAD_W = 2 * DA_HEAD_DIM
ONES_ROWS = 16
ATT_TK = 512
SUB_BLK = 16
SUB_BLK_F32 = 8


def _cp(*sem):
    return pltpu.CompilerParams(dimension_semantics=sem, vmem_limit_bytes=V7X_VMEM_LIMIT_BYTES)


def _dot(a, b):
    return jnp.dot(a, b, preferred_element_type=F32)


def _split(a):
    hi = a.astype(BF16)
    lo = (a - hi.astype(F32)).astype(BF16)
    return hi, lo


def _dot3(a, b):
    ah, al = _split(a)
    bh, bl = _split(b)
    return _dot(ah, bh) + _dot(al, bh) + _dot(ah, bl)


def _ln(x):
    mu = jnp.mean(x, axis=-1, keepdims=True)
    xc = x - mu
    var = jnp.mean(xc * xc, axis=-1, keepdims=True)
    return xc * lax.rsqrt(var + LN_EPS)


def _pick(n, prefs):
    for p in prefs:
        if n % p == 0:
            return p
    return n


def _mod_kernel(cv_ref, w_ref, b_ref, o_ref):
    cv = cv_ref[...]
    s = cv * jax.nn.sigmoid(cv)
    o_ref[0] = _dot3(s, w_ref[0]) + b_ref[0]


def _mod_params(cv, mod_w, mod_b):
    depth, d, n = mod_w.shape
    rows = cv.shape[0]
    tn = _pick(n, (1536, 1024, 512, 256, 128))
    return pl.pallas_call(
        _mod_kernel,
        grid=(depth, n // tn),
        in_specs=[pl.BlockSpec((rows, d), lambda i, j: (0, 0)),
                  pl.BlockSpec((1, d, tn), lambda i, j: (i, 0, j)),
                  pl.BlockSpec((1, 1, tn), lambda i, j: (i, 0, j))],
        out_specs=pl.BlockSpec((1, rows, tn), lambda i, j: (i, 0, j)),
        out_shape=jax.ShapeDtypeStruct((depth, rows, n), F32),
        compiler_params=_cp("parallel", "parallel"),
        name="mod_params",
    )(cv, mod_w, mod_b.reshape(depth, 1, n))


def _rope_block(blk, cos, sin_a, sin_b):
    return (blk * cos + pltpu.roll(blk, V7X_LANES - 16, 1) * sin_a + pltpu.roll(blk, 16, 1) * sin_b)


def _store_vT(v, vT_ref, head):
    vT_ref[0, head, 0, 0:HEAD_W, :] = v.T.astype(BF16)
    vT_ref[0, head, 0, HEAD_W:, :] = jnp.ones((ONES_ROWS, v.shape[0]), BF16)


def _inproj_kernel(x_ref, sh_ref, sc_ref, w_ref, cos_ref, sa_ref, sb_ref,
                   qT_ref, k_ref, vT_ref, f_ref, *, da_w, q_scale):
    h = (_ln(x_ref[0]) * (1.0 + sc_ref[0]) + sh_ref[0]).astype(BF16)
    p = _dot(h, w_ref[...])
    cos, sa, sb = cos_ref[...], sa_ref[...], sb_ref[...]
    for j in range(da_w // HEAD_W):
        lo, hi = j * HEAD_W, (j + 1) * HEAD_W
        qT_ref[0, lo:hi, :] = (_rope_block(p[:, lo:hi], cos, sa, sb) * q_scale).T.astype(BF16)
        k_ref[0, :, lo:hi] = _rope_block(p[:, da_w + lo:da_w + hi], cos, sa, sb).astype(BF16)
        _store_vT(p[:, 2 * da_w + lo:2 * da_w + hi], vT_ref, j)
    f_ref[0] = p[:, 3 * da_w:].astype(BF16)


def _inproj(x, sh, sc, w, cos, sa, sb, da_w):
    b, l, d = x.shape
    n = w.shape[1]
    fn_w = n - 3 * da_w
    heads = da_w // HEAD_W
    tm = ATT_TK
    row = lambda bi, i: (bi, i, 0)
    vec = lambda bi, i: (bi, 0, 0)
    tab = lambda bi, i: (i, 0)
    return pl.pallas_call(
        functools.partial(_inproj_kernel, da_w=da_w, q_scale=DA_HEAD_DIM ** -0.5 * math.log2(math.e)),
        grid=(b, l // tm),
        in_specs=[pl.BlockSpec((1, tm, d), row),
                  pl.BlockSpec((1, 1, d), vec), pl.BlockSpec((1, 1, d), vec),
                  pl.BlockSpec((d, n), lambda bi, i: (0, 0)),
                  pl.BlockSpec((tm, V7X_LANES), tab), pl.BlockSpec((tm, V7X_LANES), tab),
                  pl.BlockSpec((tm, V7X_LANES), tab)],
        out_specs=[pl.BlockSpec((1, da_w, tm), lambda bi, i: (bi, 0, i)),
                   pl.BlockSpec((1, tm, da_w), row),
                   pl.BlockSpec((1, heads, 1, HEAD_W + ONES_ROWS, tm), lambda bi, i: (bi, 0, i, 0, 0)),
                   pl.BlockSpec((1, tm, fn_w), row)],
        out_shape=[jax.ShapeDtypeStruct((b, da_w, l), BF16), jax.ShapeDtypeStruct((b, l, da_w), BF16),
                   jax.ShapeDtypeStruct((b, heads, l // tm, HEAD_W + ONES_ROWS, tm), BF16),
                   jax.ShapeDtypeStruct((b, l, fn_w), BF16)],
        compiler_params=_cp("parallel", "parallel"),
        name="l0_inproj",
    )(x, sh, sc, w, cos, sa, sb)


def _ctx_kv_kernel(x_ref, sh_ref, sc_ref, w_ref, k_ref, vT_ref, *, da_w):
    h = (_ln(x_ref[0]) * (1.0 + sc_ref[0]) + sh_ref[0]).astype(BF16)
    p = _dot(h, w_ref[...])
    k_ref[0] = p[:, :da_w].astype(BF16)
    for j in range(da_w // HEAD_W):
        _store_vT(p[:, da_w + j * HEAD_W:da_w + (j + 1) * HEAD_W], vT_ref, j)


def _ctx_kv(ctx, sh, sc, w, da_w):
    b, lc, d = ctx.shape
    heads = da_w // HEAD_W
    return pl.pallas_call(
        functools.partial(_ctx_kv_kernel, da_w=da_w),
        grid=(b,),
        in_specs=[pl.BlockSpec((1, lc, d), lambda bi: (bi, 0, 0)),
                  pl.BlockSpec((1, 1, d), lambda bi: (0, 0, 0)),
                  pl.BlockSpec((1, 1, d), lambda bi: (0, 0, 0)),
                  pl.BlockSpec((d, 2 * da_w), lambda bi: (0, 0))],
        out_specs=[pl.BlockSpec((1, lc, da_w), lambda bi: (bi, 0, 0)),
                   pl.BlockSpec((1, heads, 1, HEAD_W + ONES_ROWS, lc), lambda bi: (bi, 0, 0, 0, 0))],
        out_shape=[jax.ShapeDtypeStruct((b, lc, da_w), BF16),
                   jax.ShapeDtypeStruct((b, heads, 1, HEAD_W + ONES_ROWS, lc), BF16)],
        compiler_params=_cp("parallel"),
        name="ctx_kv_proj",
    )(ctx, sh, sc, w)


def _attn_kernel(qT_ref, kc_ref, vcT_ref, k_ref, vT_ref, lam_ref, g_ref, o_ref,
                 m1_ref, a1_ref, m2_ref, a2_ref, sa1_ref, sa2_ref, sb1_ref, sb2_ref, *, lam_init):
    qT = qT_ref[0]
    row = lax.broadcasted_iota(jnp.int32, qT.shape, 0)
    zero = jnp.zeros_like(qT)
    q1 = jnp.where(row < DA_HEAD_DIM, qT, zero)
    q2 = jnp.where(row >= DA_HEAD_DIM, qT, zero)
    nk = vT_ref.shape[2]

    m1_ref[...] = jnp.full_like(m1_ref, -jnp.inf)
    m2_ref[...] = jnp.full_like(m2_ref, -jnp.inf)
    a1_ref[...] = jnp.zeros_like(a1_ref)
    a2_ref[...] = jnp.zeros_like(a2_ref)

    def softmax_pv(s, vb, m_ref, a_ref):
        m_old = m_ref[...]
        m_new = jnp.maximum(m_old, jnp.max(s, axis=0, keepdims=True))
        p = jnp.exp2(s - m_new).astype(BF16)
        a_ref[...] = a_ref[...] * jnp.exp2(m_old - m_new) + _dot(vb, p)
        m_ref[...] = m_new

    def scores(t, s1_ref, s2_ref):
        off = pl.multiple_of(t * ATT_TK, ATT_TK)
        kb = k_ref[0, pl.ds(off, ATT_TK), :]
        s1_ref[...] = _dot(kb, q1)
        s2_ref[...] = _dot(kb, q2)

    def consume(t, s1_ref, s2_ref):
        vb = vT_ref[0, 0, t]
        softmax_pv(s1_ref[...], vb, m1_ref, a1_ref)
        softmax_pv(s2_ref[...], vb, m2_ref, a2_ref)

    kc, vc = kc_ref[0], vcT_ref[0, 0, 0]
    softmax_pv(_dot(kc, q1), vc, m1_ref, a1_ref)
    softmax_pv(_dot(kc, q2), vc, m2_ref, a2_ref)

    scores(0, sa1_ref, sa2_ref)

    def body(j, carry):
        t = 2 * j
        scores(t + 1, sb1_ref, sb2_ref)
        consume(t, sa1_ref, sa2_ref)
        scores(jnp.minimum(t + 2, nk - 1), sa1_ref, sa2_ref)
        consume(t + 1, sb1_ref, sb2_ref)
        return carry

    lax.fori_loop(0, nk // 2, body, 0)

    lp = lam_ref[...]
    lam = (jnp.exp(jnp.sum(lp[0:1] * lp[1:2], axis=-1, keepdims=True))
           - jnp.exp(jnp.sum(lp[2:3] * lp[3:4], axis=-1, keepdims=True)) + lam_init)
    a1, a2 = a1_ref[...], a2_ref[...]
    o = (a1[:HEAD_W] * (1.0 / a1[HEAD_W:HEAD_W + 1])
         - lam * (a2[:HEAD_W] * (1.0 / a2[HEAD_W:HEAD_W + 1])))
    ms = jnp.mean(o * o, axis=0, keepdims=True)
    on = (o * lax.rsqrt(ms + LN_EPS)).T
    o_ref[0] = (on * g_ref[...] * (1.0 - lam_init)).astype(o_ref.dtype)


def _diff_attention(qT, k, vT, kc, vcT, lam_rows, subln_g, lam_init):
    b, da_w, l = qT.shape
    lc = kc.shape[1]
    heads = da_w // HEAD_W
    nk = vT.shape[2]
    av = HEAD_W + ONES_ROWS
    tq = _pick(l, (512, 256, 128))
    kmap = lambda bi, h, i: (bi, 0, h)
    vmap = lambda bi, h, i: (bi, h, 0, 0, 0)
    return pl.pallas_call(
        functools.partial(_attn_kernel, lam_init=lam_init),
        grid=(b, heads, l // tq),
        in_specs=[pl.BlockSpec((1, HEAD_W, tq), lambda bi, h, i: (bi, h, i)),
                  pl.BlockSpec((1, lc, HEAD_W), kmap),
                  pl.BlockSpec((1, 1, 1, av, lc), vmap),
                  pl.BlockSpec((1, l, HEAD_W), kmap),
                  pl.BlockSpec((1, 1, nk, av, ATT_TK), vmap),
                  pl.BlockSpec((8, V7X_LANES), lambda bi, h, i: (0, 0)),
                  pl.BlockSpec((1, HEAD_W), lambda bi, h, i: (0, 0))],
        out_specs=pl.BlockSpec((1, tq, HEAD_W), lambda bi, h, i: (bi, i, h)),
        out_shape=jax.ShapeDtypeStruct((b, l, da_w), BF16),
        scratch_shapes=[pltpu.VMEM((1, tq), F32), pltpu.VMEM((av, tq), F32),
                        pltpu.VMEM((1, tq), F32), pltpu.VMEM((av, tq), F32)]
                       + [pltpu.VMEM((ATT_TK, tq), F32)] * 4,
        compiler_params=_cp("parallel", "parallel", "parallel"),
        name="diff_attention",
    )(qT, kc, vcT, k, vT, lam_rows, subln_g)


def _fourier_tables(l, fn_w):
    r = l // DFT_INNER
    gw = fn_w // FN_GROUPS
    cc = np.arange(gw)
    th = 2.0 * np.pi * np.outer(cc, cc) / gw
    eye = np.eye(FN_GROUPS)
    wc = np.kron(eye, np.cos(th) / np.sqrt(gw))
    ws = np.kron(eye, -np.sin(th) / np.sqrt(gw))
    k1 = np.arange(r)[None, :, None]
    n1 = np.arange(r)[None, None, :]
    n2 = np.arange(DFT_INNER)[:, None, None]
    th1 = 2.0 * np.pi * ((k1 * (DFT_INNER * n1 + n2)) % l) / l
    gr, gi = np.cos(th1), -np.sin(th1)
    g = np.concatenate([np.concatenate([gr, -gi], 2), np.concatenate([gi, gr], 2)], 1)
    k2 = np.arange(DFT_INNER)
    th3 = 2.0 * np.pi * np.outer(k2, k2) / DFT_INNER
    m3 = np.concatenate([np.cos(th3), np.sin(th3)], 1) / np.sqrt(l)
    m3 = np.kron(np.eye(2), m3)
    as_bf = lambda a: jnp.asarray(a, dtype=F32).astype(BF16)
    return as_bf(wc), as_bf(ws), as_bf(g), as_bf(m3)


def _swap01(x):
    return jnp.swapaxes(x, 0, 1)


def _fm1_kernel(x_ref, wc_ref, ws_ref, g_ref, o_ref):
    x = _swap01(x_ref[0].astype(F32))
    for r in range(SUB_BLK):
        xr = x[r].astype(BF16)
        zs = jnp.concatenate([_dot(xr, wc_ref[...]), _dot(xr, ws_ref[...])], axis=0).astype(BF16)
        o_ref[0, r] = _dot(g_ref[r], zs).astype(o_ref.dtype)


def _fm2_kernel(ar_ref, ai_ref, m3_ref, o_ref):
    ar = _swap01(ar_ref[0].astype(F32))
    ai = _swap01(ai_ref[0].astype(F32))
    outs = []
    for q in range(SUB_BLK // 2):
        rhs = jnp.concatenate([ar[2 * q], ai[2 * q], ar[2 * q + 1], ai[2 * q + 1]], axis=0).astype(BF16)
        o2 = _dot(m3_ref[...], rhs)
        outs += [o2[:DFT_INNER], o2[DFT_INNER:]]
    o_ref[0] = _swap01(jnp.stack(outs, axis=0)).astype(o_ref.dtype)


def _fourier_mix(f, tables):
    b, l, fn_w = f.shape
    r = l // DFT_INNER
    wc, ws, g, m3 = tables
    const = lambda bi, j: (0, 0)
    a = pl.pallas_call(
        _fm1_kernel,
        grid=(b, DFT_INNER // SUB_BLK),
        in_specs=[pl.BlockSpec((1, r, SUB_BLK, fn_w), lambda bi, j: (bi, 0, j, 0)),
                  pl.BlockSpec((fn_w, fn_w), const), pl.BlockSpec((fn_w, fn_w), const),
                  pl.BlockSpec((SUB_BLK, 2 * r, 2 * r), lambda bi, j: (j, 0, 0))],
        out_specs=pl.BlockSpec((1, SUB_BLK, 2 * r, fn_w), lambda bi, j: (bi, j, 0, 0)),
        out_shape=jax.ShapeDtypeStruct((b, DFT_INNER, 2 * r, fn_w), BF16),
        compiler_params=_cp("parallel", "parallel"),
        name="fourier_stage1",
    )(f.reshape(b, r, DFT_INNER, fn_w), wc, ws, g)
    nkb = r // SUB_BLK
    out = pl.pallas_call(
        _fm2_kernel,
        grid=(b, nkb),
        in_specs=[pl.BlockSpec((1, DFT_INNER, SUB_BLK, fn_w), lambda bi, j: (bi, 0, j, 0)),
                  pl.BlockSpec((1, DFT_INNER, SUB_BLK, fn_w), lambda bi, j: (bi, 0, nkb + j, 0)),
                  pl.BlockSpec((2 * DFT_INNER, 4 * DFT_INNER), const)],
        out_specs=pl.BlockSpec((1, DFT_INNER, SUB_BLK, fn_w), lambda bi, j: (bi, 0, j, 0)),
        out_shape=jax.ShapeDtypeStruct((b, DFT_INNER, r, fn_w), BF16),
        compiler_params=_cp("parallel", "parallel"),
        name="fourier_stage2",
    )(a, a, m3)
    return out.reshape(b, l, fn_w)


def _proj_res_ln_kernel(*refs, n_in, alpha):
    ins = refs[:n_in]
    ws = refs[n_in:2 * n_in]
    x_ref, gate_ref, g_ref, b_ref, o_ref = refs[2 * n_in:]
    y = _dot(ins[0][0].astype(BF16), ws[0][...])
    for a_ref, w_ref in zip(ins[1:], ws[1:]):
        y = y + _dot(a_ref[0].astype(BF16), w_ref[...])
    z = alpha * x_ref[0] + gate_ref[0] * y
    o_ref[0] = _ln(z) * g_ref[...] + b_ref[...]


def _proj_res_ln(acts, weights, x, gate, ln_g, ln_b, alpha, name):
    b, l, d = x.shape
    tm = _pick(l, (512, 256, 128))
    row = lambda bi, i: (bi, i, 0)
    const = lambda bi, i: (0, 0)
    in_specs = ([pl.BlockSpec((1, tm, a.shape[2]), row) for a in acts]
                + [pl.BlockSpec(w.shape, const) for w in weights]
                + [pl.BlockSpec((1, tm, d), row), pl.BlockSpec((1, 1, d), lambda bi, i: (bi, 0, 0)),
                   pl.BlockSpec((1, d), const), pl.BlockSpec((1, d), const)])
    return pl.pallas_call(
        functools.partial(_proj_res_ln_kernel, n_in=len(acts), alpha=alpha),
        grid=(b, l // tm),
        in_specs=in_specs,
        out_specs=pl.BlockSpec((1, tm, d), row),
        out_shape=jax.ShapeDtypeStruct((b, l, d), F32),
        compiler_params=_cp("parallel", "parallel"),
        name=name,
    )(*acts, *weights, x, gate, ln_g.reshape(1, d), ln_b.reshape(1, d))


def _halo_ln(xp_ref, x_ref, xn_ref, sh_ref, sc_ref, first, last):
    mod = lambda v: _ln(v) * (1.0 + sc_ref[0]) + sh_ref[0]
    hp = mod(xp_ref[0]) * jnp.where(first, 0.0, 1.0)
    hn = mod(xn_ref[0]) * jnp.where(last, 0.0, 1.0)
    return jnp.concatenate([hp, mod(x_ref[0]), hn], axis=0).astype(BF16)


def _dwconv3(pre, cw, cb):
    n_rows = pre.shape[0]
    up = pltpu.roll(pre, 1, 0)
    dn = pltpu.roll(pre, n_rows - 1, 0)
    u = up * cw[0:1] + pre * cw[1:2] + dn * cw[2:3] + cb
    return u[HALO:n_rows - HALO]


def _halo_specs(tm, d, l):
    nb = tm // HALO
    last_blk = l // HALO - 1
    return [pl.BlockSpec((1, HALO, d), lambda bi, i, *_: (bi, jnp.maximum(i * nb - 1, 0), 0)),
            pl.BlockSpec((1, tm, d), lambda bi, i, *_: (bi, i, 0)),
            pl.BlockSpec((1, HALO, d), lambda bi, i, *_: (bi, jnp.minimum((i + 1) * nb, last_blk), 0))]


def _erf(x):
    return lax.erf(x)


def _ffn_kernel(xp_ref, x_ref, xn_ref, sh_ref, sc_ref, gate_ref,
                wa_ref, wg_ref, cwa_ref, cwg_ref, cba_ref, cbg_ref, wo_ref, g_ref, b_ref,
                o_ref, h_ref, acc_ref, *, alpha):
    i, j = pl.program_id(1), pl.program_id(2)

    @pl.when(j == 0)
    def _():
        h_ref[...] = _halo_ln(xp_ref, x_ref, xn_ref, sh_ref, sc_ref,
                              i == 0, i == pl.num_programs(1) - 1)
        acc_ref[...] = jnp.zeros_like(acc_ref)

    h = h_ref[...]
    a = _dwconv3(_dot(h, wa_ref[...]), cwa_ref[...], cba_ref[...])
    g = _dwconv3(_dot(h, wg_ref[...]), cwg_ref[...], cbg_ref[...])
    z = (0.5 * a * (1.0 + _erf(a * (2.0 ** -0.5))) * g).astype(BF16)
    acc_ref[...] += _dot(z, wo_ref[...])

    @pl.when(j == pl.num_programs(2) - 1)
    def _():
        zz = alpha * x_ref[0] + gate_ref[0] * acc_ref[...]
        o_ref[0] = _ln(zz) * g_ref[...] + b_ref[...]


def _conv_ffn(x, sh, sc, gate, w_in, conv_w, conv_b, w_out, ln_g, ln_b, alpha, name):
    b, l, d = x.shape
    f = w_out.shape[0]
    tm = _pick(l, (1024, 512, 256, 128))
    tf = _pick(f, (256, 128))
    nf = f // tf
    vec = lambda bi, i, j: (bi, 0, 0)
    const = lambda bi, i, j: (0, 0)
    cb2 = conv_b.reshape(1, 2 * f)
    return pl.pallas_call(
        functools.partial(_ffn_kernel, alpha=alpha),
        grid=(b, l // tm, nf),
        in_specs=_halo_specs(tm, d, l) + [
            pl.BlockSpec((1, 1, d), vec), pl.BlockSpec((1, 1, d), vec), pl.BlockSpec((1, 1, d), vec),
            pl.BlockSpec((d, tf), lambda bi, i, j: (0, j)),
            pl.BlockSpec((d, tf), lambda bi, i, j: (0, nf + j)),
            pl.BlockSpec((3, tf), lambda bi, i, j: (0, j)),
            pl.BlockSpec((3, tf), lambda bi, i, j: (0, nf + j)),
            pl.BlockSpec((1, tf), lambda bi, i, j: (0, j)),
            pl.BlockSpec((1, tf), lambda bi, i, j: (0, nf + j)),
            pl.BlockSpec((tf, d), lambda bi, i, j: (j, 0)),
            pl.BlockSpec((1, d), const), pl.BlockSpec((1, d), const)],
        out_specs=pl.BlockSpec((1, tm, d), lambda bi, i, j: (bi, i, 0)),
        out_shape=jax.ShapeDtypeStruct((b, l, d), F32),
        scratch_shapes=[pltpu.VMEM((tm + 2 * HALO, d), BF16), pltpu.VMEM((tm, d), F32)],
        compiler_params=_cp("parallel", "parallel", "arbitrary"),
        name=name,
    )(x, x, x, sh, sc, gate, w_in, w_in, conv_w, conv_w, cb2, cb2, w_out,
      ln_g.reshape(1, d), ln_b.reshape(1, d))


def _hy_inproj_kernel(xp_ref, x_ref, xn_ref, sh_ref, sc_ref, w_ref, cw_ref, cb_ref,
                      x0_ref, vx_ref, *, d, tn):
    i = pl.program_id(1)
    h = _halo_ln(xp_ref, x_ref, xn_ref, sh_ref, sc_ref, i == 0, i == pl.num_programs(1) - 1)
    for c in range(d // tn):
        def part(k):
            lo = k * d + c * tn
            return _dwconv3(_dot(h, w_ref[:, lo:lo + tn]), cw_ref[:, lo:lo + tn], cb_ref[:, lo:lo + tn])
        x0_ref[0, :, c * tn:(c + 1) * tn] = part(0)
        vx_ref[0, :, c * tn:(c + 1) * tn] = part(2) * part(1)


def _hy_inproj(x, sh, sc, w, conv_w, conv_b):
    b, l, d = x.shape
    tm = _pick(l, (512, 256, 128))
    tn = _pick(d, (256, 128))
    vec = lambda bi, i: (bi, 0, 0)
    const = lambda bi, i: (0, 0)
    return pl.pallas_call(
        functools.partial(_hy_inproj_kernel, d=d, tn=tn),
        grid=(b, l // tm),
        in_specs=_halo_specs(tm, d, l) + [
            pl.BlockSpec((1, 1, d), vec), pl.BlockSpec((1, 1, d), vec),
            pl.BlockSpec((d, 3 * d), const), pl.BlockSpec((3, 3 * d), const),
            pl.BlockSpec((1, 3 * d), const)],
        out_specs=[pl.BlockSpec((1, tm, d), lambda bi, i: (bi, i, 0)),
                   pl.BlockSpec((1, tm, d), lambda bi, i: (bi, i, 0))],
        out_shape=[jax.ShapeDtypeStruct((b, l, d), F32), jax.ShapeDtypeStruct((b, l, d), F32)],
        compiler_params=_cp("parallel", "parallel"),
        name="hyena_inproj",
    )(x, x, x, sh, sc, w, conv_w, conv_b.reshape(1, 3 * d))


def _hy_filter_kernel(z_ref, w1_ref, b1_ref, w2_ref, b2_ref, w3_ref, b3_ref, fr_ref, w4_ref,
                      t_ref, dl_ref, hfb_ref, inv_ref, *, d):
    i = pl.program_id(0)
    fr = fr_ref[...]
    hdn = jnp.sin(fr * (_dot3(z_ref[...], w1_ref[...]) + b1_ref[...]))
    hdn = jnp.sin(fr * (_dot3(hdn, w2_ref[...]) + b2_ref[...]))
    hdn = jnp.sin(fr * (_dot3(hdn, w3_ref[...]) + b3_ref[...]))
    h = _dot3(hdn, w4_ref[...])
    decay = jnp.exp(-t_ref[...] * dl_ref[...])
    hf = h[:, :d] * decay
    hb = h[:, d:] * decay
    row = lax.broadcasted_iota(jnp.int32, hb.shape, 0)
    hb = jnp.where((row == 0) & (i == 0), 0.0, hb)
    hfb_ref[0] = hf
    hfb_ref[1] = hb

    @pl.when(i == 0)
    def _():
        inv_ref[...] = jnp.zeros_like(inv_ref)

    inv_ref[...] += (jnp.sum(jnp.abs(hf), axis=0, keepdims=True)
                     + jnp.sum(jnp.abs(hb), axis=0, keepdims=True))

    @pl.when(i == pl.num_programs(0) - 1)
    def _():
        inv_ref[...] = 1.0 / inv_ref[...]


def _hy_filters(l, d, w1, b1, w2, b2, w3, b3, freq, w4):
    hh = w1.shape[1]
    bands = (HY_FILTER_EMB - 1) // 2
    t = np.linspace(0.0, 1.0, l, dtype=np.float32)[:, None]
    w = (2.0 * math.pi * np.arange(l, dtype=np.float32)[:, None] / l).astype(np.float32)
    fb = np.linspace(1e-4, bands - 1, bands, dtype=np.float32)[None, :]
    z = np.concatenate([t, np.cos(fb * w), -np.sin(fb * w)], axis=-1).astype(np.float32)
    zp = np.zeros((l, V7X_LANES), np.float32)
    zp[:, :HY_FILTER_EMB] = z
    w1p = jnp.zeros((V7X_LANES, hh), F32).at[:HY_FILTER_EMB].set(w1.astype(F32))
    min_decay = math.log(HY_DECAY_TARGET) / HY_SLOW_DECAY
    max_decay = math.log(HY_DECAY_TARGET) / HY_FAST_DECAY
    deltas = np.abs(np.linspace(min_decay, max_decay, d, dtype=np.float32))[None, :]
    tl = _pick(l, (512, 256, 128))
    const = lambda i: (0, 0)
    r2 = lambda a: a.astype(F32).reshape(1, -1)
    return pl.pallas_call(
        functools.partial(_hy_filter_kernel, d=d),
        grid=(l // tl,),
        in_specs=[pl.BlockSpec((tl, V7X_LANES), lambda i: (i, 0)),
                  pl.BlockSpec((V7X_LANES, hh), const), pl.BlockSpec((1, hh), const),
                  pl.BlockSpec((hh, hh), const), pl.BlockSpec((1, hh), const),
                  pl.BlockSpec((hh, hh), const), pl.BlockSpec((1, hh), const),
                  pl.BlockSpec((1, hh), const), pl.BlockSpec((hh, 2 * d), const),
                  pl.BlockSpec((tl, 1), lambda i: (i, 0)), pl.BlockSpec((1, d), const)],
        out_specs=[pl.BlockSpec((2, tl, d), lambda i: (0, i, 0)), pl.BlockSpec((1, d), const)],
        out_shape=[jax.ShapeDtypeStruct((2, l, d), F32), jax.ShapeDtypeStruct((1, d), F32)],
        compiler_params=_cp("arbitrary"),
        name="hyena_filter",
    )(jnp.asarray(zp), w1p, r2(b1), w2.astype(F32), r2(b2), w3.astype(F32), r2(b3), r2(freq),
      w4.astype(F32), jnp.asarray(t), jnp.asarray(deltas))


def _hyena_tables(l):
    r = l // DFT_INNER
    n = 2 * l
    k1 = np.arange(2 * r)[None, :, None]
    n1 = np.arange(r)[None, None, :]
    n2 = np.arange(DFT_INNER)[:, None, None]
    th = 2.0 * np.pi * ((k1 * (DFT_INNER * n1 + n2)) % n) / n
    gr, gi = np.cos(th), -np.sin(th)
    g1r = np.concatenate([gr, gi], 1)
    g1c = np.concatenate([np.concatenate([gr, -gi], 2), np.concatenate([gi, gr], 2)], 1)
    k2 = np.arange(DFT_INNER)
    th3 = 2.0 * np.pi * np.outer(k2, k2) / DFT_INNER
    c, s = np.cos(th3), np.sin(th3)
    m3f = np.kron(np.eye(2), np.block([[c, s], [-s, c]]))
    m3i = np.kron(np.eye(2), np.block([[c, -s], [s, c]]))
    thi = np.transpose(th, (0, 2, 1))
    hr, hi = np.cos(thi), np.sin(thi)
    hc = np.concatenate([np.concatenate([hr, -hi], 2), np.concatenate([hi, hr], 2)], 1) / n
    as_bf = lambda a: jnp.asarray(a, dtype=F32).astype(BF16)
    return as_bf(g1r), as_bf(g1c), as_bf(m3f), as_bf(m3i), as_bf(hc)


def _hy1_kernel(v_ref, g_ref, ar_ref, ai_ref, *, r2):
    xa = _swap01(v_ref[0, 0])
    xb = _swap01(v_ref[0, 1])
    for s in range(SUB_BLK_F32):
        z = jnp.concatenate([xa[s], xb[s]], axis=0).astype(BF16)
        a = _dot(g_ref[s], z)
        ar_ref[0, s] = a[:r2].astype(BF16)
        ai_ref[0, s] = a[r2:].astype(BF16)


def _hy1f_kernel(v_ref, g_ref, ar_ref, ai_ref, *, r2):
    x = _swap01(v_ref[0])
    for s in range(SUB_BLK_F32):
        a = _dot(g_ref[s], x[s].astype(BF16))
        ar_ref[0, s] = a[:r2].astype(BF16)
        ai_ref[0, s] = a[r2:].astype(BF16)


def _hy_stage1(v5, g, td, real_input):
    r, d = v5.shape[-3], v5.shape[-1]
    nseq = v5.shape[0]
    nb = DFT_INNER // SUB_BLK_F32
    if real_input:
        vspec = pl.BlockSpec((1, r, SUB_BLK_F32, td), lambda p, j, h: (p, 0, j, h))
        body = _hy1f_kernel
    else:
        vspec = pl.BlockSpec((1, 2, r, SUB_BLK_F32, td), lambda p, j, h: (p, 0, 0, j, h))
        body = _hy1_kernel
    ospec = pl.BlockSpec((1, SUB_BLK_F32, 2 * r, td), lambda p, j, h: (p, j, 0, h))
    oshape = jax.ShapeDtypeStruct((nseq, DFT_INNER, 2 * r, d), BF16)
    return pl.pallas_call(
        functools.partial(body, r2=2 * r),
        grid=(nseq, nb, d // td),
        in_specs=[vspec, pl.BlockSpec((SUB_BLK_F32,) + g.shape[1:], lambda p, j, h: (j, 0, 0))],
        out_specs=[ospec, ospec],
        out_shape=[oshape, oshape],
        compiler_params=_cp("parallel", "parallel", "parallel"),
        name="hyena_fft_stage1_filter" if real_input else "hyena_fft_stage1",
    )(v5, g)


def _k1_pair_rhs(fr, fi, q):
    return jnp.concatenate([fr[2 * q], fi[2 * q], fr[2 * q + 1], fi[2 * q + 1]], axis=0).astype(BF16)


def _hy2f_kernel(fr_ref, fi_ref, inv_ref, m3f_ref, kr_ref, ki_ref):
    h = DFT_INNER
    frh, fih = _swap01(fr_ref[0].astype(F32)), _swap01(fi_ref[0].astype(F32))
    frb, fib = _swap01(fr_ref[1].astype(F32)), _swap01(fi_ref[1].astype(F32))
    inv = inv_ref[...]
    for q in range(SUB_BLK // 2):
        sh = _dot(m3f_ref[...], _k1_pair_rhs(frh, fih, q))
        sb = _dot(m3f_ref[...], _k1_pair_rhs(frb, fib, q))
        for u in range(2):
            o = 2 * h * u
            kr_ref[2 * q + u] = ((sh[o:o + h] + sb[o:o + h]) * inv).astype(BF16)
            ki_ref[2 * q + u] = ((sh[o + h:o + 2 * h] - sb[o + h:o + 2 * h]) * inv).astype(BF16)


def _hy_filter_spectrum(afr, afi, inv_norm, m3f, td):
    _, _, r2, d = afr.shape
    fspec = pl.BlockSpec((2, DFT_INNER, SUB_BLK, td), lambda j, h: (0, 0, j, h))
    kspec = pl.BlockSpec((SUB_BLK, DFT_INNER, td), lambda j, h: (j, 0, h))
    kshape = jax.ShapeDtypeStruct((r2, DFT_INNER, d), BF16)
    return pl.pallas_call(
        _hy2f_kernel,
        grid=(r2 // SUB_BLK, d // td),
        in_specs=[fspec, fspec, pl.BlockSpec((1, td), lambda j, h: (0, h)),
                  pl.BlockSpec(m3f.shape, lambda j, h: (0, 0))],
        out_specs=[kspec, kspec],
        out_shape=[kshape, kshape],
        compiler_params=_cp("parallel", "parallel"),
        name="hyena_filter_spectrum",
    )(afr, afi, inv_norm, m3f)


def _hy2_kernel(ar_ref, ai_ref, kr_ref, ki_ref, m3f_ref, m3i_ref, br_ref, bi_ref):
    h = DFT_INNER
    ar, ai = _swap01(ar_ref[0].astype(F32)), _swap01(ai_ref[0].astype(F32))
    brs, bis = [], []
    for q in range(SUB_BLK // 2):
        v = _dot(m3f_ref[...], _k1_pair_rhs(ar, ai, q))
        ys = []
        for u in range(2):
            o = 2 * h * u
            vr, vi = v[o:o + h], v[o + h:o + 2 * h]
            kr, ki = kr_ref[2 * q + u].astype(F32), ki_ref[2 * q + u].astype(F32)
            ys += [vr * kr - vi * ki, vr * ki + vi * kr]
        bn = _dot(m3i_ref[...], jnp.concatenate(ys, axis=0).astype(BF16))
        brs += [bn[0:h], bn[2 * h:3 * h]]
        bis += [bn[h:2 * h], bn[3 * h:4 * h]]
    br_ref[0] = _swap01(jnp.stack(brs, axis=0)).astype(BF16)
    bi_ref[0] = _swap01(jnp.stack(bis, axis=0)).astype(BF16)


def _hy_stage2(ar, ai, kr, ki, m3f, m3i, td):
    p, _, r2, d = ar.shape
    aspec = pl.BlockSpec((1, DFT_INNER, SUB_BLK, td), lambda j, h, pi: (pi, 0, j, h))
    kspec = pl.BlockSpec((SUB_BLK, DFT_INNER, td), lambda j, h, pi: (j, 0, h))
    mspec = pl.BlockSpec(m3f.shape, lambda j, h, pi: (0, 0))
    oshape = jax.ShapeDtypeStruct(ar.shape, BF16)
    return pl.pallas_call(
        _hy2_kernel,
        grid=(r2 // SUB_BLK, d // td, p),
        in_specs=[aspec, aspec, kspec, kspec, mspec, mspec],
        out_specs=[aspec, aspec],
        out_shape=[oshape, oshape],
        compiler_params=_cp("parallel", "parallel", "parallel"),
        name="hyena_fft_stage2",
    )(ar, ai, kr, ki, m3f, m3i)


def _hy3_kernel(br_ref, bi_ref, hc_ref, vx_ref, x0_ref, d_ref, o_ref, *, r):
    ya, yb = [], []
    for s in range(SUB_BLK_F32):
        bb = jnp.concatenate([br_ref[0, s], bi_ref[0, s]], axis=0)
        y = _dot(hc_ref[s], bb)
        ya.append(y[:r])
        yb.append(y[r:])
    for u, ys in enumerate((ya, yb)):
        y = _swap01(jnp.stack(ys, axis=0))
        o_ref[0, u] = (y + vx_ref[0, u] * d_ref[...]) * x0_ref[0, u]


def _hy_stage3(br, bi, hc, vx5, x05, d_skip, td):
    p, _, r, _, d = vx5.shape
    bspec = pl.BlockSpec((1, SUB_BLK_F32, 2 * r, td), lambda pi, j, h: (pi, j, 0, h))
    vspec = pl.BlockSpec((1, 2, r, SUB_BLK_F32, td), lambda pi, j, h: (pi, 0, 0, j, h))
    return pl.pallas_call(
        functools.partial(_hy3_kernel, r=r),
        grid=(p, DFT_INNER // SUB_BLK_F32, d // td),
        in_specs=[bspec, bspec,
                  pl.BlockSpec((SUB_BLK_F32, 2 * r, 4 * r), lambda pi, j, h: (j, 0, 0)),
                  vspec, vspec, pl.BlockSpec((1, 1, td), lambda pi, j, h: (0, 0, h))],
        out_specs=vspec,
        out_shape=jax.ShapeDtypeStruct(vx5.shape, F32),
        compiler_params=_cp("parallel", "parallel", "parallel"),
        name="hyena_fft_stage3",
    )(br, bi, hc, vx5, x05, d_skip.astype(F32).reshape(1, 1, d))


def _hyena_long_conv(vx, x0, hfb, inv_norm, d_skip, tables):
    b, l, d = vx.shape
    r = l // DFT_INNER
    g1r, g1c, m3f, m3i, hc = tables
    td = _pick(d, (512, 256, 128))
    pair = lambda a: a.reshape(b // 2, 2, r, DFT_INNER, d)
    ar, ai = _hy_stage1(pair(vx), g1c, td, real_input=False)
    afr, afi = _hy_stage1(hfb.reshape(2, r, DFT_INNER, d), g1r, td, real_input=True)
    kr, ki = _hy_filter_spectrum(afr, afi, inv_norm, m3f, td)
    br, bi = _hy_stage2(ar, ai, kr, ki, m3f, m3i, td)
    return _hy_stage3(br, bi, hc, pair(vx), pair(x0), d_skip, td).reshape(b, l, d)


def _rope_tables(l):
    rows = l // GRID_W
    axis_dim = DA_HEAD_DIM // 2
    pos_r = jnp.repeat(jnp.arange(rows, dtype=F32), GRID_W)
    pos_c = jnp.tile(jnp.arange(GRID_W, dtype=F32), rows)
    inv = ROPE_THETA ** (-jnp.arange(0, axis_dim, 2, dtype=F32) / axis_dim)
    ar, ac = pos_r[:, None] * inv, pos_c[:, None] * inv
    n = ar.shape[1]
    zeros = jnp.zeros((l, n), F32)
    cos64 = jnp.concatenate([jnp.cos(ar), jnp.cos(ar), jnp.cos(ac), jnp.cos(ac)], -1)
    sa64 = jnp.concatenate([-jnp.sin(ar), zeros, -jnp.sin(ac), zeros], -1)
    sb64 = jnp.concatenate([zeros, jnp.sin(ar), zeros, jnp.sin(ac)], -1)
    rep = V7X_LANES // DA_HEAD_DIM
    return jnp.tile(cos64, (1, rep)), jnp.tile(sa64, (1, rep)), jnp.tile(sb64, (1, rep))


def _lambda_init(layer_idx):
    return 0.8 - 0.6 * math.exp(-0.3 * layer_idx)


def kernel(x, c, ctx, c_ctx, mod_w, mod_b, ln1_g, ln1_b, ln2_g, ln2_b, ffn_w_in, ffn_conv_w, ffn_conv_b, ffn_w_out, da_w_in, da_w_out, da_lam_q1, da_lam_k1, da_lam_q2, da_lam_k2, da_subln_g, hy_w_in, hy_conv_w, hy_conv_b, hy_f_w1, hy_f_b1, hy_f_w2, hy_f_b2, hy_f_w3, hy_f_b3, hy_f_freq, hy_f_w4, hy_d, hy_w_out):
    b, l, d = x.shape
    depth = mod_w.shape[0]
    assert depth == 2 and l % ATT_TK == 0 and d % (2 * V7X_LANES) == 0
    assert b % 2 == 0 and (l // DFT_INNER) % SUB_BLK == 0
    da_w = d // 2
    alpha = (2 * depth) ** 0.25

    rows = -(-(b + 1) // 8) * 8
    cv = jnp.zeros((rows, d), F32).at[:b].set(c.astype(F32)).at[b].set(c_ctx.astype(F32))
    mod = _mod_params(cv, mod_w.astype(F32), mod_b.astype(F32))

    def mods(i):
        m = mod[i].reshape(rows, N_MOD, d)
        return [m[:b, k][:, None, :] for k in range(N_MOD)], [m[b:b + 1, k][:, None, :] for k in range(N_MOD)]

    (sh1, sc1, g1, sh2, sc2, g2), (csh1, csc1, _, _, _, _) = mods(0)
    w_in = da_w_in[0].astype(BF16)
    cos, sa, sb = _rope_tables(l)
    qT, k, vT, f = _inproj(x, sh1, sc1, w_in, cos, sa, sb, da_w)
    kc, vcT = _ctx_kv(ctx, csh1, csc1, w_in[:, da_w:3 * da_w], da_w)
    lam_rows = jnp.zeros((8, V7X_LANES), F32)
    for r_i, p in enumerate((da_lam_q1[0], da_lam_k1[0], da_lam_q2[0], da_lam_k2[0])):
        lam_rows = lam_rows.at[r_i, :DA_HEAD_DIM].set(p.astype(F32))
    o = _diff_attention(qT, k, vT, kc, vcT, lam_rows, da_subln_g[0].astype(F32).reshape(1, -1), _lambda_init(0))
    fm = _fourier_mix(f, _fourier_tables(l, d - da_w))
    w_out = da_w_out[0].astype(BF16)
    x = _proj_res_ln([o, fm], [w_out[:da_w], w_out[da_w:]], x, g1, ln1_g[0], ln1_b[0], alpha, "l0_outproj")
    x = _conv_ffn(x, sh2, sc2, g2, ffn_w_in[0].astype(BF16), ffn_conv_w[0].astype(F32),
                  ffn_conv_b[0].astype(F32), ffn_w_out[0].astype(BF16), ln2_g[0], ln2_b[0], alpha, "l0_ffn")

    (sh1, sc1, g1, sh2, sc2, g2), _ = mods(1)
    x0, vx = _hy_inproj(x, sh1, sc1, hy_w_in[0].astype(BF16), hy_conv_w[0].astype(F32),
                        hy_conv_b[0].astype(F32))
    hfb, inv_norm = _hy_filters(l, d, hy_f_w1[0], hy_f_b1[0], hy_f_w2[0], hy_f_b2[0],
                                hy_f_w3[0], hy_f_b3[0], hy_f_freq[0], hy_f_w4[0])
    z = _hyena_long_conv(vx, x0, hfb, inv_norm, hy_d[0], _hyena_tables(l))
    x = _proj_res_ln([z], [hy_w_out[0].astype(BF16)], x, g1, ln1_g[1], ln1_b[1], alpha, "l1_outproj")
    x = _conv_ffn(x, sh2, sc2, g2, ffn_w_in[1].astype(BF16), ffn_conv_w[1].astype(F32),
                  ffn_conv_b[1].astype(F32), ffn_w_out[1].astype(BF16), ln2_g[1], ln2_b[1], alpha, "l1_ffn")
    return x
```

```python
import functools
import math

import numpy as np
import jax
import jax.numpy as jnp
from jax import lax
from jax.experimental import pallas as pl
from jax.experimental.pallas import tpu as pltpu

F32 = jnp.float32
BF16 = jnp.bfloat16

GRID_W = 64
DA_HEAD_DIM = 64
ROPE_THETA = 10000.0
N_MOD = 6
LN_EPS = 1e-5
FN_GROUPS = 4
HY_FILTER_EMB = 33
HY_DECAY_TARGET = 1e-2
HY_FAST_DECAY = 0.3
HY_SLOW_DECAY = 1.5
DFT_INNER = GRID_W

V7X_LANES = 128
V7X_VMEM_LIMIT_BYTES = 56 * 1024 * 1024
HALO = 8
HEAD_W = 2 * DA_HEAD_DIM
ONES_ROWS = 16
ATT_TK = 512
ATT_UNROLL = 8
SUB_BLK = 16
SUB_BLK_F32 = 8


def _cp(*sem):
    return pltpu.CompilerParams(dimension_semantics=sem, vmem_limit_bytes=V7X_VMEM_LIMIT_BYTES)


def _dot(a, b):
    return jnp.dot(a, b, preferred_element_type=F32)


def _split(a):
    hi = a.astype(BF16)
    lo = (a - hi.astype(F32)).astype(BF16)
    return hi, lo


def _dot3(a, b):
    ah, al = _split(a)
    bh, bl = _split(b)
    return _dot(ah, bh) + _dot(al, bh) + _dot(ah, bl)


def _ln(x):
    mu = jnp.mean(x, axis=-1, keepdims=True)
    xc = x - mu
    var = jnp.mean(xc * xc, axis=-1, keepdims=True)
    return xc * lax.rsqrt(var + LN_EPS)


def _pick(n, prefs):
    for p in prefs:
        if n % p == 0:
            return p
    return n


def _mod_kernel(cv_ref, w_ref, b_ref, o_ref):
    cv = cv_ref[...]
    s = cv * jax.nn.sigmoid(cv)
    o_ref[0] = _dot3(s, w_ref[0]) + b_ref[0]


def _mod_params(cv, mod_w, mod_b):
    depth, d, n = mod_w.shape
    rows = cv.shape[0]
    tn = _pick(n, (1536, 1024, 512, 256, 128))
    return pl.pallas_call(
        _mod_kernel,
        grid=(depth, n // tn),
        in_specs=[pl.BlockSpec((rows, d), lambda i, j: (0, 0)),
                  pl.BlockSpec((1, d, tn), lambda i, j: (i, 0, j)),
                  pl.BlockSpec((1, 1, tn), lambda i, j: (i, 0, j))],
        out_specs=pl.BlockSpec((1, rows, tn), lambda i, j: (i, 0, j)),
        out_shape=jax.ShapeDtypeStruct((depth, rows, n), F32),
        compiler_params=_cp("parallel", "parallel"),
        name="mod_params",
    )(cv, mod_w, mod_b.reshape(depth, 1, n))


def _rope_block(blk, cos, sin_a, sin_b):
    return (blk * cos + pltpu.roll(blk, V7X_LANES - 16, 1) * sin_a + pltpu.roll(blk, 16, 1) * sin_b)


def _store_vT(v, vT_ref, head):
    vT_ref[0, head, 0, 0:HEAD_W, :] = v.T.astype(BF16)
    vT_ref[0, head, 0, HEAD_W:, :] = jnp.ones((ONES_ROWS, v.shape[0]), BF16)


def _inproj_kernel(x_ref, sh_ref, sc_ref, w_ref, cos_ref, sa_ref, sb_ref,
                   qT_ref, k_ref, vT_ref, f_ref, *, da_w, q_scale):
    h = (_ln(x_ref[0]) * (1.0 + sc_ref[0]) + sh_ref[0]).astype(BF16)
    p = _dot(h, w_ref[...])
    cos, sa, sb = cos_ref[...], sa_ref[...], sb_ref[...]
    for j in range(da_w // HEAD_W):
        lo, hi = j * HEAD_W, (j + 1) * HEAD_W
        qT_ref[0, lo:hi, :] = (_rope_block(p[:, lo:hi], cos, sa, sb) * q_scale).T.astype(BF16)
        k_ref[0, :, lo:hi] = _rope_block(p[:, da_w + lo:da_w + hi], cos, sa, sb).astype(BF16)
        _store_vT(p[:, 2 * da_w + lo:2 * da_w + hi], vT_ref, j)
    f_ref[0] = p[:, 3 * da_w:].astype(BF16)


def _inproj(x, sh, sc, w, cos, sa, sb, da_w):
    b, l, d = x.shape
    n = w.shape[1]
    fn_w = n - 3 * da_w
    heads = da_w // HEAD_W
    tm = ATT_TK
    row = lambda bi, i: (bi, i, 0)
    vec = lambda bi, i: (bi, 0, 0)
    tab = lambda bi, i: (i, 0)
    return pl.pallas_call(
        functools.partial(_inproj_kernel, da_w=da_w, q_scale=DA_HEAD_DIM ** -0.5 * math.log2(math.e)),
        grid=(b, l // tm),
        in_specs=[pl.BlockSpec((1, tm, d), row),
                  pl.BlockSpec((1, 1, d), vec), pl.BlockSpec((1, 1, d), vec),
                  pl.BlockSpec((d, n), lambda bi, i: (0, 0)),
                  pl.BlockSpec((tm, V7X_LANES), tab), pl.BlockSpec((tm, V7X_LANES), tab),
                  pl.BlockSpec((tm, V7X_LANES), tab)],
        out_specs=[pl.BlockSpec((1, da_w, tm), lambda bi, i: (bi, 0, i)),
                   pl.BlockSpec((1, tm, da_w), row),
                   pl.BlockSpec((1, heads, 1, HEAD_W + ONES_ROWS, tm), lambda bi, i: (bi, 0, i, 0, 0)),
                   pl.BlockSpec((1, tm, fn_w), row)],
        out_shape=[jax.ShapeDtypeStruct((b, da_w, l), BF16), jax.ShapeDtypeStruct((b, l, da_w), BF16),
                   jax.ShapeDtypeStruct((b, heads, l // tm, HEAD_W + ONES_ROWS, tm), BF16),
                   jax.ShapeDtypeStruct((b, l, fn_w), BF16)],
        compiler_params=_cp("parallel", "parallel"),
        name="l0_inproj",
    )(x, sh, sc, w, cos, sa, sb)


def _ctx_kv_kernel(x_ref, sh_ref, sc_ref, w_ref, k_ref, vT_ref, *, da_w):
    h = (_ln(x_ref[0]) * (1.0 + sc_ref[0]) + sh_ref[0]).astype(BF16)
    p = _dot(h, w_ref[...])
    k_ref[0] = p[:, :da_w].astype(BF16)
    for j in range(da_w // HEAD_W):
        _store_vT(p[:, da_w + j * HEAD_W:da_w + (j + 1) * HEAD_W], vT_ref, j)


def _ctx_kv(ctx, sh, sc, w, da_w):
    b, lc, d = ctx.shape
    heads = da_w // HEAD_W
    return pl.pallas_call(
        functools.partial(_ctx_kv_kernel, da_w=da_w),
        grid=(b,),
        in_specs=[pl.BlockSpec((1, lc, d), lambda bi: (bi, 0, 0)),
                  pl.BlockSpec((1, 1, d), lambda bi: (0, 0, 0)),
                  pl.BlockSpec((1, 1, d), lambda bi: (0, 0, 0)),
                  pl.BlockSpec((d, 2 * da_w), lambda bi: (0, 0))],
        out_specs=[pl.BlockSpec((1, lc, da_w), lambda bi: (bi, 0, 0)),
                   pl.BlockSpec((1, heads, 1, HEAD_W + ONES_ROWS, lc), lambda bi: (bi, 0, 0, 0, 0))],
        out_shape=[jax.ShapeDtypeStruct((b, lc, da_w), BF16),
                   jax.ShapeDtypeStruct((b, heads, 1, HEAD_W + ONES_ROWS, lc), BF16)],
        compiler_params=_cp("parallel"),
        name="ctx_kv_proj",
    )(ctx, sh, sc, w)


def _attn_kernel(qT_ref, kc_ref, vcT_ref, k_ref, vT_ref, lam_ref, g_ref, o_ref,
                 m1_ref, a1_ref, m2_ref, a2_ref, sa1_ref, sa2_ref, sb1_ref, sb2_ref, *, lam_init):
    qT = qT_ref[0]
    row = lax.broadcasted_iota(jnp.int32, qT.shape, 0)
    zero = jnp.zeros_like(qT)
    q1 = jnp.where(row < DA_HEAD_DIM, qT, zero)
    q2 = jnp.where(row >= DA_HEAD_DIM, qT, zero)
    nk = vT_ref.shape[2]

    m1_ref[...] = jnp.full_like(m1_ref, -jnp.inf)
    m2_ref[...] = jnp.full_like(m2_ref, -jnp.inf)
    a1_ref[...] = jnp.zeros_like(a1_ref)
    a2_ref[...] = jnp.zeros_like(a2_ref)

    def softmax_pv(s, vb, m_ref, a_ref):
        m_old = m_ref[...]
        m_new = jnp.maximum(m_old, jnp.max(s, axis=0, keepdims=True))
        p = jnp.exp2(s - m_new).astype(BF16)
        a_ref[...] = a_ref[...] * jnp.exp2(m_old - m_new) + _dot(vb, p)
        m_ref[...] = m_new

    def scores(t, s1_ref, s2_ref):
        off = pl.multiple_of(t * ATT_TK, ATT_TK)
        kb = k_ref[0, pl.ds(off, ATT_TK), :]
        s1_ref[...] = _dot(kb, q1)
        s2_ref[...] = _dot(kb, q2)

    def consume(t, s1_ref, s2_ref):
        vb = vT_ref[0, 0, t]
        softmax_pv(s1_ref[...], vb, m1_ref, a1_ref)
        softmax_pv(s2_ref[...], vb, m2_ref, a2_ref)

    kc, vc = kc_ref[0], vcT_ref[0, 0, 0]
    sc1, sc2 = _dot(kc, q1), _dot(kc, q2)
    scores(0, sa1_ref, sa2_ref)
    softmax_pv(sc1, vc, m1_ref, a1_ref)
    softmax_pv(sc2, vc, m2_ref, a2_ref)

    bufs = ((sa1_ref, sa2_ref), (sb1_ref, sb2_ref))

    unroll = math.gcd(nk, ATT_UNROLL)
    assert unroll % 2 == 0

    def body(j, carry):
        for u in range(unroll):
            t = unroll * j + u
            scores(jnp.minimum(t + 1, nk - 1), *bufs[(u + 1) % 2])
            consume(t, *bufs[u % 2])
        return carry

    lax.fori_loop(0, nk // unroll, body, 0)

    lp = lam_ref[...]
    lam = (jnp.exp(jnp.sum(lp[0:1] * lp[1:2], axis=-1, keepdims=True))
           - jnp.exp(jnp.sum(lp[2:3] * lp[3:4], axis=-1, keepdims=True)) + lam_init)
    a1, a2 = a1_ref[...], a2_ref[...]
    o = (a1[:HEAD_W] * (1.0 / a1[HEAD_W:HEAD_W + 1])
         - lam * (a2[:HEAD_W] * (1.0 / a2[HEAD_W:HEAD_W + 1])))
    ms = jnp.mean(o * o, axis=0, keepdims=True)
    on = (o * lax.rsqrt(ms + LN_EPS)).T
    o_ref[0] = (on * g_ref[...] * (1.0 - lam_init)).astype(o_ref.dtype)


def _diff_attention(qT, k, vT, kc, vcT, lam_rows, subln_g, lam_init):
    b, da_w, l = qT.shape
    lc = kc.shape[1]
    heads = da_w // HEAD_W
    nk = vT.shape[2]
    av = HEAD_W + ONES_ROWS
    tq = _pick(l, (512, 256, 128))
    kmap = lambda bi, h, i: (bi, 0, h)
    vmap = lambda bi, h, i: (bi, h, 0, 0, 0)
    return pl.pallas_call(
        functools.partial(_attn_kernel, lam_init=lam_init),
        grid=(b, heads, l // tq),
        in_specs=[pl.BlockSpec((1, HEAD_W, tq), lambda bi, h, i: (bi, h, i)),
                  pl.BlockSpec((1, lc, HEAD_W), kmap),
                  pl.BlockSpec((1, 1, 1, av, lc), vmap),
                  pl.BlockSpec((1, l, HEAD_W), kmap),
                  pl.BlockSpec((1, 1, nk, av, ATT_TK), vmap),
                  pl.BlockSpec((8, V7X_LANES), lambda bi, h, i: (0, 0)),
                  pl.BlockSpec((1, HEAD_W), lambda bi, h, i: (0, 0))],
        out_specs=pl.BlockSpec((1, tq, HEAD_W), lambda bi, h, i: (bi, i, h)),
        out_shape=jax.ShapeDtypeStruct((b, l, da_w), BF16),
        scratch_shapes=[pltpu.VMEM((1, tq), F32), pltpu.VMEM((av, tq), F32),
                        pltpu.VMEM((1, tq), F32), pltpu.VMEM((av, tq), F32)]
                       + [pltpu.VMEM((ATT_TK, tq), F32)] * 4,
        compiler_params=_cp("parallel", "parallel", "parallel"),
        name="diff_attention",
    )(qT, kc, vcT, k, vT, lam_rows, subln_g)


def _fourier_tables(l, fn_w):
    r = l // DFT_INNER
    gw = fn_w // FN_GROUPS
    cc = np.arange(gw)
    th = 2.0 * np.pi * np.outer(cc, cc) / gw
    eye = np.eye(FN_GROUPS)
    wc = np.kron(eye, np.cos(th) / np.sqrt(gw))
    ws = np.kron(eye, -np.sin(th) / np.sqrt(gw))
    k1 = np.arange(r)[None, :, None]
    n1 = np.arange(r)[None, None, :]
    n2 = np.arange(DFT_INNER)[:, None, None]
    th1 = 2.0 * np.pi * ((k1 * (DFT_INNER * n1 + n2)) % l) / l
    gr, gi = np.cos(th1), -np.sin(th1)
    g = np.concatenate([np.concatenate([gr, -gi], 2), np.concatenate([gi, gr], 2)], 1)
    k2 = np.arange(DFT_INNER)
    th3 = 2.0 * np.pi * np.outer(k2, k2) / DFT_INNER
    m3 = np.concatenate([np.cos(th3), np.sin(th3)], 1) / np.sqrt(l)
    m3 = np.kron(np.eye(2), m3)
    as_bf = lambda a: jnp.asarray(a, dtype=F32).astype(BF16)
    return as_bf(wc), as_bf(ws), as_bf(g), as_bf(m3)


def _swap01(x):
    return jnp.swapaxes(x, 0, 1)


def _fm1_kernel(x_ref, wc_ref, ws_ref, g_ref, o_ref):
    x = _swap01(x_ref[0].astype(F32))
    for r in range(SUB_BLK):
        xr = x[r].astype(BF16)
        zs = jnp.concatenate([_dot(xr, wc_ref[...]), _dot(xr, ws_ref[...])], axis=0).astype(BF16)
        o_ref[0, r] = _dot(g_ref[r], zs).astype(o_ref.dtype)


def _fm2_kernel(ar_ref, ai_ref, m3_ref, o_ref):
    ar = _swap01(ar_ref[0].astype(F32))
    ai = _swap01(ai_ref[0].astype(F32))
    outs = []
    for q in range(SUB_BLK // 2):
        rhs = jnp.concatenate([ar[2 * q], ai[2 * q], ar[2 * q + 1], ai[2 * q + 1]], axis=0).astype(BF16)
        o2 = _dot(m3_ref[...], rhs)
        outs += [o2[:DFT_INNER], o2[DFT_INNER:]]
    o_ref[0] = _swap01(jnp.stack(outs, axis=0)).astype(o_ref.dtype)


def _fourier_mix(f, tables):
    b, l, fn_w = f.shape
    r = l // DFT_INNER
    wc, ws, g, m3 = tables
    const = lambda bi, j: (0, 0)
    a = pl.pallas_call(
        _fm1_kernel,
        grid=(b, DFT_INNER // SUB_BLK),
        in_specs=[pl.BlockSpec((1, r, SUB_BLK, fn_w), lambda bi, j: (bi, 0, j, 0)),
                  pl.BlockSpec((fn_w, fn_w), const), pl.BlockSpec((fn_w, fn_w), const),
                  pl.BlockSpec((SUB_BLK, 2 * r, 2 * r), lambda bi, j: (j, 0, 0))],
        out_specs=pl.BlockSpec((1, SUB_BLK, 2 * r, fn_w), lambda bi, j: (bi, j, 0, 0)),
        out_shape=jax.ShapeDtypeStruct((b, DFT_INNER, 2 * r, fn_w), BF16),
        compiler_params=_cp("parallel", "parallel"),
        name="fourier_stage1",
    )(f.reshape(b, r, DFT_INNER, fn_w), wc, ws, g)
    nkb = r // SUB_BLK
    out = pl.pallas_call(
        _fm2_kernel,
        grid=(b, nkb),
        in_specs=[pl.BlockSpec((1, DFT_INNER, SUB_BLK, fn_w), lambda bi, j: (bi, 0, j, 0)),
                  pl.BlockSpec((1, DFT_INNER, SUB_BLK, fn_w), lambda bi, j: (bi, 0, nkb + j, 0)),
                  pl.BlockSpec((2 * DFT_INNER, 4 * DFT_INNER), const)],
        out_specs=pl.BlockSpec((1, DFT_INNER, SUB_BLK, fn_w), lambda bi, j: (bi, 0, j, 0)),
        out_shape=jax.ShapeDtypeStruct((b, DFT_INNER, r, fn_w), BF16),
        compiler_params=_cp("parallel", "parallel"),
        name="fourier_stage2",
    )(a, a, m3)
    return out.reshape(b, l, fn_w)


def _proj_res_ln_kernel(*refs, n_in, alpha):
    ins = refs[:n_in]
    ws = refs[n_in:2 * n_in]
    x_ref, gate_ref, g_ref, b_ref, o_ref = refs[2 * n_in:]
    y = _dot(ins[0][0].astype(BF16), ws[0][...])
    for a_ref, w_ref in zip(ins[1:], ws[1:]):
        y = y + _dot(a_ref[0].astype(BF16), w_ref[...])
    z = alpha * x_ref[0] + gate_ref[0] * y
    o_ref[0] = _ln(z) * g_ref[...] + b_ref[...]


def _proj_res_ln(acts, weights, x, gate, ln_g, ln_b, alpha, name):
    b, l, d = x.shape
    tm = _pick(l, (512, 256, 128))
    row = lambda bi, i: (bi, i, 0)
    const = lambda bi, i: (0, 0)
    in_specs = ([pl.BlockSpec((1, tm, a.shape[2]), row) for a in acts]
                + [pl.BlockSpec(w.shape, const) for w in weights]
                + [pl.BlockSpec((1, tm, d), row), pl.BlockSpec((1, 1, d), lambda bi, i: (bi, 0, 0)),
                   pl.BlockSpec((1, d), const), pl.BlockSpec((1, d), const)])
    return pl.pallas_call(
        functools.partial(_proj_res_ln_kernel, n_in=len(acts), alpha=alpha),
        grid=(b, l // tm),
        in_specs=in_specs,
        out_specs=pl.BlockSpec((1, tm, d), row),
        out_shape=jax.ShapeDtypeStruct((b, l, d), F32),
        compiler_params=_cp("parallel", "parallel"),
        name=name,
    )(*acts, *weights, x, gate, ln_g.reshape(1, d), ln_b.reshape(1, d))


def _halo_ln(xp_ref, x_ref, xn_ref, sh_ref, sc_ref, first, last):
    mod = lambda v: _ln(v) * (1.0 + sc_ref[0]) + sh_ref[0]
    hp = mod(xp_ref[0]) * jnp.where(first, 0.0, 1.0)
    hn = mod(xn_ref[0]) * jnp.where(last, 0.0, 1.0)
    return jnp.concatenate([hp, mod(x_ref[0]), hn], axis=0).astype(BF16)


def _dwconv3(pre, cw, cb):
    n_rows = pre.shape[0]
    up = pltpu.roll(pre, 1, 0)
    dn = pltpu.roll(pre, n_rows - 1, 0)
    u = up * cw[0:1] + pre * cw[1:2] + dn * cw[2:3] + cb
    return u[HALO:n_rows - HALO]


def _halo_specs(tm, d, l):
    nb = tm // HALO
    last_blk = l // HALO - 1
    return [pl.BlockSpec((1, HALO, d), lambda bi, i, *_: (bi, jnp.maximum(i * nb - 1, 0), 0)),
            pl.BlockSpec((1, tm, d), lambda bi, i, *_: (bi, i, 0)),
            pl.BlockSpec((1, HALO, d), lambda bi, i, *_: (bi, jnp.minimum((i + 1) * nb, last_blk), 0))]


def _erf(x):
    return lax.erf(x)


def _ffn_kernel(xp_ref, x_ref, xn_ref, sh_ref, sc_ref, gate_ref, wi_ref, cw_ref, cb_ref, wo_ref,
                g_ref, b_ref, o_ref, acc_ref, *, alpha, f, tf):
    i = pl.program_id(1)
    h = _halo_ln(xp_ref, x_ref, xn_ref, sh_ref, sc_ref, i == 0, i == pl.num_programs(1) - 1)
    nf = f // tf

    def project(j):
        return tuple(_dot(h, wi_ref[:, lo:lo + tf]) for lo in (j * tf, f + j * tf))

    pre = project(0)
    for j in range(nf):
        nxt = project(j + 1) if j + 1 < nf else None
        a, g = (_dwconv3(p, cw_ref[:, lo:lo + tf], cb_ref[:, lo:lo + tf])
                for p, lo in zip(pre, (j * tf, f + j * tf)))
        z = (0.5 * a * (1.0 + _erf(a * (2.0 ** -0.5))) * g).astype(BF16)
        contrib = _dot(z, wo_ref[j * tf:(j + 1) * tf, :])
        if j == 0:
            acc_ref[...] = contrib
        else:
            acc_ref[...] += contrib
        pre = nxt
    zz = alpha * x_ref[0] + gate_ref[0] * acc_ref[...]
    o_ref[0] = _ln(zz) * g_ref[...] + b_ref[...]


def _conv_ffn(x, sh, sc, gate, w_in, conv_w, conv_b, w_out, ln_g, ln_b, alpha, name):
    b, l, d = x.shape
    f = w_out.shape[0]
    tm = _pick(l, (512, 256, 128))
    tf = _pick(f, (256, 128))
    vec = lambda bi, i: (bi, 0, 0)
    resident = lambda shape: pl.BlockSpec(shape, lambda bi, i: (0, 0), pipeline_mode=pl.Buffered(1))
    return pl.pallas_call(
        functools.partial(_ffn_kernel, alpha=alpha, f=f, tf=tf),
        grid=(b, l // tm),
        in_specs=_halo_specs(tm, d, l) + [
            pl.BlockSpec((1, 1, d), vec), pl.BlockSpec((1, 1, d), vec), pl.BlockSpec((1, 1, d), vec),
            resident((d, 2 * f)), resident((3, 2 * f)), resident((1, 2 * f)), resident((f, d)),
            resident((1, d)), resident((1, d))],
        out_specs=pl.BlockSpec((1, tm, d), lambda bi, i: (bi, i, 0)),
        out_shape=jax.ShapeDtypeStruct((b, l, d), F32),
        scratch_shapes=[pltpu.VMEM((tm, d), F32)],
        compiler_params=_cp("parallel", "parallel"),
        name=name,
    )(x, x, x, sh, sc, gate, w_in, conv_w, conv_b.reshape(1, 2 * f), w_out,
      ln_g.reshape(1, d), ln_b.reshape(1, d))


def _hy_inproj_kernel(xp_ref, x_ref, xn_ref, sh_ref, sc_ref, w_ref, cw_ref, cb_ref,
                      x0_ref, vx_ref, *, d, tn):
    i = pl.program_id(1)
    h = _halo_ln(xp_ref, x_ref, xn_ref, sh_ref, sc_ref, i == 0, i == pl.num_programs(1) - 1)
    for c in range(d // tn):
        def part(k):
            lo = k * d + c * tn
            return _dwconv3(_dot(h, w_ref[:, lo:lo + tn]), cw_ref[:, lo:lo + tn], cb_ref[:, lo:lo + tn])
        x0_ref[0, :, c * tn:(c + 1) * tn] = part(0)
        vx_ref[0, :, c * tn:(c + 1) * tn] = part(2) * part(1)


def _hy_inproj(x, sh, sc, w, conv_w, conv_b):
    b, l, d = x.shape
    tm = _pick(l, (512, 256, 128))
    tn = _pick(d, (256, 128))
    vec = lambda bi, i: (bi, 0, 0)
    const = lambda bi, i: (0, 0)
    return pl.pallas_call(
        functools.partial(_hy_inproj_kernel, d=d, tn=tn),
        grid=(b, l // tm),
        in_specs=_halo_specs(tm, d, l) + [
            pl.BlockSpec((1, 1, d), vec), pl.BlockSpec((1, 1, d), vec),
            pl.BlockSpec((d, 3 * d), const), pl.BlockSpec((3, 3 * d), const),
            pl.BlockSpec((1, 3 * d), const)],
        out_specs=[pl.BlockSpec((1, tm, d), lambda bi, i: (bi, i, 0)),
                   pl.BlockSpec((1, tm, d), lambda bi, i: (bi, i, 0))],
        out_shape=[jax.ShapeDtypeStruct((b, l, d), F32), jax.ShapeDtypeStruct((b, l, d), F32)],
        compiler_params=_cp("parallel", "parallel"),
        name="hyena_inproj",
    )(x, x, x, sh, sc, w, conv_w, conv_b.reshape(1, 3 * d))


def _hy_filter_kernel(z_ref, w1_ref, b1_ref, w2_ref, b2_ref, w3_ref, b3_ref, fr_ref, w4_ref,
                      t_ref, dl_ref, hfb_ref, inv_ref, *, d):
    i = pl.program_id(0)
    fr = fr_ref[...]
    hdn = jnp.sin(fr * (_dot3(z_ref[...], w1_ref[...]) + b1_ref[...]))
    hdn = jnp.sin(fr * (_dot3(hdn, w2_ref[...]) + b2_ref[...]))
    hdn = jnp.sin(fr * (_dot3(hdn, w3_ref[...]) + b3_ref[...]))
    h = _dot3(hdn, w4_ref[...])
    decay = jnp.exp(-t_ref[...] * dl_ref[...])
    hf = h[:, :d] * decay
    hb = h[:, d:] * decay
    row = lax.broadcasted_iota(jnp.int32, hb.shape, 0)
    hb = jnp.where((row == 0) & (i == 0), 0.0, hb)
    hfb_ref[0] = hf
    hfb_ref[1] = hb

    @pl.when(i == 0)
    def _():
        inv_ref[...] = jnp.zeros_like(inv_ref)

    inv_ref[...] += (jnp.sum(jnp.abs(hf), axis=0, keepdims=True)
                     + jnp.sum(jnp.abs(hb), axis=0, keepdims=True))

    @pl.when(i == pl.num_programs(0) - 1)
    def _():
        inv_ref[...] = 1.0 / inv_ref[...]


def _hy_filters(l, d, w1, b1, w2, b2, w3, b3, freq, w4):
    hh = w1.shape[1]
    bands = (HY_FILTER_EMB - 1) // 2
    t = np.linspace(0.0, 1.0, l, dtype=np.float32)[:, None]
    w = (2.0 * math.pi * np.arange(l, dtype=np.float32)[:, None] / l).astype(np.float32)
    fb = np.linspace(1e-4, bands - 1, bands, dtype=np.float32)[None, :]
    z = np.concatenate([t, np.cos(fb * w), -np.sin(fb * w)], axis=-1).astype(np.float32)
    zp = np.zeros((l, V7X_LANES), np.float32)
    zp[:, :HY_FILTER_EMB] = z
    w1p = jnp.zeros((V7X_LANES, hh), F32).at[:HY_FILTER_EMB].set(w1.astype(F32))
    min_decay = math.log(HY_DECAY_TARGET) / HY_SLOW_DECAY
    max_decay = math.log(HY_DECAY_TARGET) / HY_FAST_DECAY
    deltas = np.abs(np.linspace(min_decay, max_decay, d, dtype=np.float32))[None, :]
    tl = _pick(l, (512, 256, 128))
    const = lambda i: (0, 0)
    r2 = lambda a: a.astype(F32).reshape(1, -1)
    return pl.pallas_call(
        functools.partial(_hy_filter_kernel, d=d),
        grid=(l // tl,),
        in_specs=[pl.BlockSpec((tl, V7X_LANES), lambda i: (i, 0)),
                  pl.BlockSpec((V7X_LANES, hh), const), pl.BlockSpec((1, hh), const),
                  pl.BlockSpec((hh, hh), const), pl.BlockSpec((1, hh), const),
                  pl.BlockSpec((hh, hh), const), pl.BlockSpec((1, hh), const),
                  pl.BlockSpec((1, hh), const), pl.BlockSpec((hh, 2 * d), const),
                  pl.BlockSpec((tl, 1), lambda i: (i, 0)), pl.BlockSpec((1, d), const)],
        out_specs=[pl.BlockSpec((2, tl, d), lambda i: (0, i, 0)), pl.BlockSpec((1, d), const)],
        out_shape=[jax.ShapeDtypeStruct((2, l, d), F32), jax.ShapeDtypeStruct((1, d), F32)],
        compiler_params=_cp("arbitrary"),
        name="hyena_filter",
    )(jnp.asarray(zp), w1p, r2(b1), w2.astype(F32), r2(b2), w3.astype(F32), r2(b3), r2(freq),
      w4.astype(F32), jnp.asarray(t), jnp.asarray(deltas))


def _hyena_tables(l):
    r = l // DFT_INNER
    n = 2 * l
    k1 = np.arange(2 * r)[None, :, None]
    n1 = np.arange(r)[None, None, :]
    n2 = np.arange(DFT_INNER)[:, None, None]
    th = 2.0 * np.pi * ((k1 * (DFT_INNER * n1 + n2)) % n) / n
    gr, gi = np.cos(th), -np.sin(th)
    g1r = np.concatenate([gr, gi], 1)
    g1c = np.concatenate([np.concatenate([gr, -gi], 2), np.concatenate([gi, gr], 2)], 1)
    k2 = np.arange(DFT_INNER)
    th3 = 2.0 * np.pi * np.outer(k2, k2) / DFT_INNER
    c, s = np.cos(th3), np.sin(th3)
    m3f = np.kron(np.eye(2), np.block([[c, s], [-s, c]]))
    m3i = np.kron(np.eye(2), np.block([[c, -s], [s, c]]))
    thi = np.transpose(th, (0, 2, 1))
    hr, hi = np.cos(thi), np.sin(thi)
    hc = np.concatenate([np.concatenate([hr, -hi], 2), np.concatenate([hi, hr], 2)], 1) / n
    as_bf = lambda a: jnp.asarray(a, dtype=F32).astype(BF16)
    return as_bf(g1r), as_bf(g1c), as_bf(m3f), as_bf(m3i), as_bf(hc)


def _hy1_kernel(v_ref, g_ref, ar_ref, ai_ref, *, r2):
    xa = _swap01(v_ref[0, 0])
    xb = _swap01(v_ref[0, 1])
    for s in range(SUB_BLK_F32):
        z = jnp.concatenate([xa[s], xb[s]], axis=0).astype(BF16)
        a = _dot(g_ref[s], z)
        ar_ref[0, s] = a[:r2].astype(BF16)
        ai_ref[0, s] = a[r2:].astype(BF16)


def _hy1f_kernel(v_ref, g_ref, ar_ref, ai_ref, *, r2):
    x = _swap01(v_ref[0])
    for s in range(SUB_BLK_F32):
        a = _dot(g_ref[s], x[s].astype(BF16))
        ar_ref[0, s] = a[:r2].astype(BF16)
        ai_ref[0, s] = a[r2:].astype(BF16)


def _hy_stage1(v5, g, td, real_input):
    r, d = v5.shape[-3], v5.shape[-1]
    nseq = v5.shape[0]
    nb = DFT_INNER // SUB_BLK_F32
    if real_input:
        vspec = pl.BlockSpec((1, r, SUB_BLK_F32, td), lambda p, j, h: (p, 0, j, h))
        body = _hy1f_kernel
    else:
        vspec = pl.BlockSpec((1, 2, r, SUB_BLK_F32, td), lambda p, j, h: (p, 0, 0, j, h))
        body = _hy1_kernel
    ospec = pl.BlockSpec((1, SUB_BLK_F32, 2 * r, td), lambda p, j, h: (p, j, 0, h))
    oshape = jax.ShapeDtypeStruct((nseq, DFT_INNER, 2 * r, d), BF16)
    return pl.pallas_call(
        functools.partial(body, r2=2 * r),
        grid=(nseq, nb, d // td),
        in_specs=[vspec, pl.BlockSpec((SUB_BLK_F32,) + g.shape[1:], lambda p, j, h: (j, 0, 0))],
        out_specs=[ospec, ospec],
        out_shape=[oshape, oshape],
        compiler_params=_cp("parallel", "parallel", "parallel"),
        name="hyena_fft_stage1_filter" if real_input else "hyena_fft_stage1",
    )(v5, g)


def _k1_pair_rhs(fr, fi, q):
    return jnp.concatenate([fr[2 * q], fi[2 * q], fr[2 * q + 1], fi[2 * q + 1]], axis=0).astype(BF16)


def _hy2f_kernel(fr_ref, fi_ref, inv_ref, m3f_ref, kr_ref, ki_ref):
    h = DFT_INNER
    frh, fih = _swap01(fr_ref[0].astype(F32)), _swap01(fi_ref[0].astype(F32))
    frb, fib = _swap01(fr_ref[1].astype(F32)), _swap01(fi_ref[1].astype(F32))
    inv = inv_ref[...]
    for q in range(SUB_BLK // 2):
        sh = _dot(m3f_ref[...], _k1_pair_rhs(frh, fih, q))
        sb = _dot(m3f_ref[...], _k1_pair_rhs(frb, fib, q))
        for u in range(2):
            o = 2 * h * u
            kr_ref[2 * q + u] = ((sh[o:o + h] + sb[o:o + h]) * inv).astype(BF16)
            ki_ref[2 * q + u] = ((sh[o + h:o + 2 * h] - sb[o + h:o + 2 * h]) * inv).astype(BF16)


def _hy_filter_spectrum(afr, afi, inv_norm, m3f, td):
    _, _, r2, d = afr.shape
    fspec = pl.BlockSpec((2, DFT_INNER, SUB_BLK, td), lambda j, h: (0, 0, j, h))
    kspec = pl.BlockSpec((SUB_BLK, DFT_INNER, td), lambda j, h: (j, 0, h))
    kshape = jax.ShapeDtypeStruct((r2, DFT_INNER, d), BF16)
    return pl.pallas_call(
        _hy2f_kernel,
        grid=(r2 // SUB_BLK, d // td),
        in_specs=[fspec, fspec, pl.BlockSpec((1, td), lambda j, h: (0, h)),
                  pl.BlockSpec(m3f.shape, lambda j, h: (0, 0))],
        out_specs=[kspec, kspec],
        out_shape=[kshape, kshape],
        compiler_params=_cp("parallel", "parallel"),
        name="hyena_filter_spectrum",
    )(afr, afi, inv_norm, m3f)


def _hy2_kernel(ar_ref, ai_ref, kr_ref, ki_ref, m3f_ref, m3i_ref, br_ref, bi_ref):
    h = DFT_INNER
    ar, ai = _swap01(ar_ref[0].astype(F32)), _swap01(ai_ref[0].astype(F32))
    brs, bis = [], []
    for q in range(SUB_BLK // 2):
        v = _dot(m3f_ref[...], _k1_pair_rhs(ar, ai, q))
        ys = []
        for u in range(2):
            o = 2 * h * u
            vr, vi = v[o:o + h], v[o + h:o + 2 * h]
            kr, ki = kr_ref[2 * q + u].astype(F32), ki_ref[2 * q + u].astype(F32)
            ys += [vr * kr - vi * ki, vr * ki + vi * kr]
        bn = _dot(m3i_ref[...], jnp.concatenate(ys, axis=0).astype(BF16))
        brs += [bn[0:h], bn[2 * h:3 * h]]
        bis += [bn[h:2 * h], bn[3 * h:4 * h]]
    br_ref[0] = _swap01(jnp.stack(brs, axis=0)).astype(BF16)
    bi_ref[0] = _swap01(jnp.stack(bis, axis=0)).astype(BF16)


def _hy_stage2(ar, ai, kr, ki, m3f, m3i, td):
    p, _, r2, d = ar.shape
    aspec = pl.BlockSpec((1, DFT_INNER, SUB_BLK, td), lambda j, h, pi: (pi, 0, j, h))
    kspec = pl.BlockSpec((SUB_BLK, DFT_INNER, td), lambda j, h, pi: (j, 0, h))
    mspec = pl.BlockSpec(m3f.shape, lambda j, h, pi: (0, 0))
    oshape = jax.ShapeDtypeStruct(ar.shape, BF16)
    return pl.pallas_call(
        _hy2_kernel,
        grid=(r2 // SUB_BLK, d // td, p),
        in_specs=[aspec, aspec, kspec, kspec, mspec, mspec],
        out_specs=[aspec, aspec],
        out_shape=[oshape, oshape],
        compiler_params=_cp("parallel", "parallel", "parallel"),
        name="hyena_fft_stage2",
    )(ar, ai, kr, ki, m3f, m3i)


def _hy3_kernel(br_ref, bi_ref, hc_ref, vx_ref, x0_ref, d_ref, o_ref, *, r):
    ya, yb = [], []
    for s in range(SUB_BLK_F32):
        bb = jnp.concatenate([br_ref[0, s], bi_ref[0, s]], axis=0)
        y = _dot(hc_ref[s], bb)
        ya.append(y[:r])
        yb.append(y[r:])
    for u, ys in enumerate((ya, yb)):
        y = _swap01(jnp.stack(ys, axis=0))
        o_ref[0, u] = (y + vx_ref[0, u] * d_ref[...]) * x0_ref[0, u]


def _hy_stage3(br, bi, hc, vx5, x05, d_skip, td):
    p, _, r, _, d = vx5.shape
    bspec = pl.BlockSpec((1, SUB_BLK_F32, 2 * r, td), lambda pi, j, h: (pi, j, 0, h))
    vspec = pl.BlockSpec((1, 2, r, SUB_BLK_F32, td), lambda pi, j, h: (pi, 0, 0, j, h))
    return pl.pallas_call(
        functools.partial(_hy3_kernel, r=r),
        grid=(p, DFT_INNER // SUB_BLK_F32, d // td),
        in_specs=[bspec, bspec,
                  pl.BlockSpec((SUB_BLK_F32, 2 * r, 4 * r), lambda pi, j, h: (j, 0, 0)),
                  vspec, vspec, pl.BlockSpec((1, 1, td), lambda pi, j, h: (0, 0, h))],
        out_specs=vspec,
        out_shape=jax.ShapeDtypeStruct(vx5.shape, F32),
        compiler_params=_cp("parallel", "parallel", "parallel"),
        name="hyena_fft_stage3",
    )(br, bi, hc, vx5, x05, d_skip.astype(F32).reshape(1, 1, d))


def _hyena_long_conv(vx, x0, hfb, inv_norm, d_skip, tables):
    b, l, d = vx.shape
    r = l // DFT_INNER
    g1r, g1c, m3f, m3i, hc = tables
    td = _pick(d, (512, 256, 128))
    pair = lambda a: a.reshape(b // 2, 2, r, DFT_INNER, d)
    ar, ai = _hy_stage1(pair(vx), g1c, td, real_input=False)
    afr, afi = _hy_stage1(hfb.reshape(2, r, DFT_INNER, d), g1r, td, real_input=True)
    kr, ki = _hy_filter_spectrum(afr, afi, inv_norm, m3f, td)
    br, bi = _hy_stage2(ar, ai, kr, ki, m3f, m3i, td)
    return _hy_stage3(br, bi, hc, pair(vx), pair(x0), d_skip, td).reshape(b, l, d)


def _rope_tables(l):
    rows = l // GRID_W
    axis_dim = DA_HEAD_DIM // 2
    pos_r = jnp.repeat(jnp.arange(rows, dtype=F32), GRID_W)
    pos_c = jnp.tile(jnp.arange(GRID_W, dtype=F32), rows)
    inv = ROPE_THETA ** (-jnp.arange(0, axis_dim, 2, dtype=F32) / axis_dim)
    ar, ac = pos_r[:, None] * inv, pos_c[:, None] * inv
    n = ar.shape[1]
    zeros = jnp.zeros((l, n), F32)
    cos64 = jnp.concatenate([jnp.cos(ar), jnp.cos(ar), jnp.cos(ac), jnp.cos(ac)], -1)
    sa64 = jnp.concatenate([-jnp.sin(ar), zeros, -jnp.sin(ac), zeros], -1)
    sb64 = jnp.concatenate([zeros, jnp.sin(ar), zeros, jnp.sin(ac)], -1)
    rep = V7X_LANES // DA_HEAD_DIM
    return jnp.tile(cos64, (1, rep)), jnp.tile(sa64, (1, rep)), jnp.tile(sb64, (1, rep))


def _lambda_init(layer_idx):
    return 0.8 - 0.6 * math.exp(-0.3 * layer_idx)


def kernel(x, c, ctx, c_ctx, mod_w, mod_b, ln1_g, ln1_b, ln2_g, ln2_b, ffn_w_in, ffn_conv_w, ffn_conv_b, ffn_w_out, da_w_in, da_w_out, da_lam_q1, da_lam_k1, da_lam_q2, da_lam_k2, da_subln_g, hy_w_in, hy_conv_w, hy_conv_b, hy_f_w1, hy_f_b1, hy_f_w2, hy_f_b2, hy_f_w3, hy_f_b3, hy_f_freq, hy_f_w4, hy_d, hy_w_out):
    b, l, d = x.shape
    depth = mod_w.shape[0]
    assert depth == 2 and l % ATT_TK == 0 and d % (2 * V7X_LANES) == 0
    assert b % 2 == 0 and (l // DFT_INNER) % SUB_BLK == 0
    da_w = d // 2
    alpha = (2 * depth) ** 0.25

    rows = -(-(b + 1) // 8) * 8
    cv = jnp.zeros((rows, d), F32).at[:b].set(c.astype(F32)).at[b].set(c_ctx.astype(F32))
    mod = _mod_params(cv, mod_w.astype(F32), mod_b.astype(F32))

    def mods(i):
        m = mod[i].reshape(rows, N_MOD, d)
        return [m[:b, k][:, None, :] for k in range(N_MOD)], [m[b:b + 1, k][:, None, :] for k in range(N_MOD)]

    (sh1, sc1, g1, sh2, sc2, g2), (csh1, csc1, _, _, _, _) = mods(0)
    w_in = da_w_in[0].astype(BF16)
    cos, sa, sb = _rope_tables(l)
    qT, k, vT, f = _inproj(x, sh1, sc1, w_in, cos, sa, sb, da_w)
    kc, vcT = _ctx_kv(ctx, csh1, csc1, w_in[:, da_w:3 * da_w], da_w)
    lam_rows = jnp.zeros((8, V7X_LANES), F32)
    for r_i, p in enumerate((da_lam_q1[0], da_lam_k1[0], da_lam_q2[0], da_lam_k2[0])):
        lam_rows = lam_rows.at[r_i, :DA_HEAD_DIM].set(p.astype(F32))
    o = _diff_attention(qT, k, vT, kc, vcT, lam_rows, da_subln_g[0].astype(F32).reshape(1, -1), _lambda_init(0))
    fm = _fourier_mix(f, _fourier_tables(l, d - da_w))
    w_out = da_w_out[0].astype(BF16)
    x = _proj_res_ln([o, fm], [w_out[:da_w], w_out[da_w:]], x, g1, ln1_g[0], ln1_b[0], alpha, "l0_outproj")
    x = _conv_ffn(x, sh2, sc2, g2, ffn_w_in[0].astype(BF16), ffn_conv_w[0].astype(F32),
                  ffn_conv_b[0].astype(F32), ffn_w_out[0].astype(BF16), ln2_g[0], ln2_b[0], alpha, "l0_ffn")

    (sh1, sc1, g1, sh2, sc2, g2), _ = mods(1)
    x0, vx = _hy_inproj(x, sh1, sc1, hy_w_in[0].astype(BF16), hy_conv_w[0].astype(F32),
                        hy_conv_b[0].astype(F32))
    hfb, inv_norm = _hy_filters(l, d, hy_f_w1[0], hy_f_b1[0], hy_f_w2[0], hy_f_b2[0],
                                hy_f_w3[0], hy_f_b3[0], hy_f_freq[0], hy_f_w4[0])
    z = _hyena_long_conv(vx, x0, hfb, inv_norm, hy_d[0], _hyena_tables(l))
    x = _proj_res_ln([z], [hy_w_out[0].astype(BF16)], x, g1, ln1_g[1], ln1_b[1], alpha, "l1_outproj")
    x = _conv_ffn(x, sh2, sc2, g2, ffn_w_in[1].astype(BF16), ffn_conv_w[1].astype(F32),
                  ffn_conv_b[1].astype(F32), ffn_w_out[1].astype(BF16), ln2_g[1], ln2_b[1], alpha, "l1_ffn")
    return x
```

```python
import functools
import math

import numpy as np
import jax
import jax.numpy as jnp
from jax import lax
from jax.experimental import pallas as pl
from jax.experimental.pallas import tpu as pltpu

F32 = jnp.float32
BF16 = jnp.bfloat16

GRID_W = 64
DA_HEAD_DIM = 64
ROPE_THETA = 10000.0
N_MOD = 6
LN_EPS = 1e-5
FN_GROUPS = 4
HY_FILTER_EMB = 33
HY_DECAY_TARGET = 1e-2
HY_FAST_DECAY = 0.3
HY_SLOW_DECAY = 1.5
DFT_INNER = GRID_W

V7X_LANES = 128
V7X_MXU_DEPTH = 256
V7X_VMEM_LIMIT_BYTES = 56 * 1024 * 1024
HALO = 8
HEAD_W = 2 * DA_HEAD_DIM
ONES_ROWS = 16
ATT_TK = 512
ATT_UNROLL = 8
SUB_BLK = 16
SUB_BLK_F32 = 8


def _cp(*sem):
    return pltpu.CompilerParams(dimension_semantics=sem, vmem_limit_bytes=V7X_VMEM_LIMIT_BYTES)


def _dot(a, b):
    return jnp.dot(a, b, preferred_element_type=F32)


def _split(a):
    hi = a.astype(BF16)
    lo = (a - hi.astype(F32)).astype(BF16)
    return hi, lo


def _dot3(a, b):
    ah, al = _split(a)
    bh, bl = _split(b)
    return _dot(ah, bh) + _dot(al, bh) + _dot(ah, bl)


def _ln(x):
    mu = jnp.mean(x, axis=-1, keepdims=True)
    xc = x - mu
    var = jnp.mean(xc * xc, axis=-1, keepdims=True)
    return xc * lax.rsqrt(var + LN_EPS)


def _pick(n, prefs):
    for p in prefs:
        if n % p == 0:
            return p
    return n


def _mod_kernel(cv_ref, w_ref, b_ref, o_ref):
    cv = cv_ref[...]
    s = cv * jax.nn.sigmoid(cv)
    o_ref[0] = _dot3(s, w_ref[0]) + b_ref[0]


def _mod_params(cv, mod_w, mod_b):
    depth, d, n = mod_w.shape
    rows = cv.shape[0]
    tn = _pick(n, (1536, 1024, 512, 256, 128))
    return pl.pallas_call(
        _mod_kernel,
        grid=(depth, n // tn),
        in_specs=[pl.BlockSpec((rows, d), lambda i, j: (0, 0)),
                  pl.BlockSpec((1, d, tn), lambda i, j: (i, 0, j)),
                  pl.BlockSpec((1, 1, tn), lambda i, j: (i, 0, j))],
        out_specs=pl.BlockSpec((1, rows, tn), lambda i, j: (i, 0, j)),
        out_shape=jax.ShapeDtypeStruct((depth, rows, n), F32),
        compiler_params=_cp("parallel", "parallel"),
        name="mod_params",
    )(cv, mod_w, mod_b.reshape(depth, 1, n))


def _rope_block(blk, cos, sin_a, sin_b):
    return (blk * cos + pltpu.roll(blk, V7X_LANES - 16, 1) * sin_a + pltpu.roll(blk, 16, 1) * sin_b)


def _store_vT(v, vT_ref, head):
    vT_ref[0, head, 0, 0:HEAD_W, :] = v.T.astype(BF16)
    vT_ref[0, head, 0, HEAD_W:, :] = jnp.ones((ONES_ROWS, v.shape[0]), BF16)


def _inproj_kernel(x_ref, sh_ref, sc_ref, w_ref, cos_ref, sa_ref, sb_ref,
                   qT_ref, k_ref, vT_ref, f_ref, *, da_w, q_scale):
    h = (_ln(x_ref[0]) * (1.0 + sc_ref[0]) + sh_ref[0]).astype(BF16)
    p = _dot(h, w_ref[...])
    cos, sa, sb = cos_ref[...], sa_ref[...], sb_ref[...]
    for j in range(da_w // HEAD_W):
        lo, hi = j * HEAD_W, (j + 1) * HEAD_W
        qT_ref[0, lo:hi, :] = (_rope_block(p[:, lo:hi], cos, sa, sb) * q_scale).T.astype(BF16)
        k_ref[0, :, lo:hi] = _rope_block(p[:, da_w + lo:da_w + hi], cos, sa, sb).astype(BF16)
        _store_vT(p[:, 2 * da_w + lo:2 * da_w + hi], vT_ref, j)
    f_ref[0] = p[:, 3 * da_w:].astype(BF16)


def _inproj(x, sh, sc, w, cos, sa, sb, da_w):
    b, l, d = x.shape
    n = w.shape[1]
    fn_w = n - 3 * da_w
    heads = da_w // HEAD_W
    tm = ATT_TK
    row = lambda bi, i: (bi, i, 0)
    vec = lambda bi, i: (bi, 0, 0)
    tab = lambda bi, i: (i, 0)
    return pl.pallas_call(
        functools.partial(_inproj_kernel, da_w=da_w, q_scale=DA_HEAD_DIM ** -0.5 * math.log2(math.e)),
        grid=(b, l // tm),
        in_specs=[pl.BlockSpec((1, tm, d), row),
                  pl.BlockSpec((1, 1, d), vec), pl.BlockSpec((1, 1, d), vec),
                  pl.BlockSpec((d, n), lambda bi, i: (0, 0)),
                  pl.BlockSpec((tm, V7X_LANES), tab), pl.BlockSpec((tm, V7X_LANES), tab),
                  pl.BlockSpec((tm, V7X_LANES), tab)],
        out_specs=[pl.BlockSpec((1, da_w, tm), lambda bi, i: (bi, 0, i)),
                   pl.BlockSpec((1, tm, da_w), row),
                   pl.BlockSpec((1, heads, 1, HEAD_W + ONES_ROWS, tm), lambda bi, i: (bi, 0, i, 0, 0)),
                   pl.BlockSpec((1, tm, fn_w), row)],
        out_shape=[jax.ShapeDtypeStruct((b, da_w, l), BF16), jax.ShapeDtypeStruct((b, l, da_w), BF16),
                   jax.ShapeDtypeStruct((b, heads, l // tm, HEAD_W + ONES_ROWS, tm), BF16),
                   jax.ShapeDtypeStruct((b, l, fn_w), BF16)],
        compiler_params=_cp("parallel", "parallel"),
        name="l0_inproj",
    )(x, sh, sc, w, cos, sa, sb)


def _ctx_kv_kernel(x_ref, sh_ref, sc_ref, w_ref, k_ref, vT_ref, *, da_w):
    h = (_ln(x_ref[0]) * (1.0 + sc_ref[0]) + sh_ref[0]).astype(BF16)
    p = _dot(h, w_ref[...])
    k_ref[0] = p[:, :da_w].astype(BF16)
    for j in range(da_w // HEAD_W):
        _store_vT(p[:, da_w + j * HEAD_W:da_w + (j + 1) * HEAD_W], vT_ref, j)


def _ctx_kv(ctx, sh, sc, w, da_w):
    b, lc, d = ctx.shape
    heads = da_w // HEAD_W
    return pl.pallas_call(
        functools.partial(_ctx_kv_kernel, da_w=da_w),
        grid=(b,),
        in_specs=[pl.BlockSpec((1, lc, d), lambda bi: (bi, 0, 0)),
                  pl.BlockSpec((1, 1, d), lambda bi: (0, 0, 0)),
                  pl.BlockSpec((1, 1, d), lambda bi: (0, 0, 0)),
                  pl.BlockSpec((d, 2 * da_w), lambda bi: (0, 0))],
        out_specs=[pl.BlockSpec((1, lc, da_w), lambda bi: (bi, 0, 0)),
                   pl.BlockSpec((1, heads, 1, HEAD_W + ONES_ROWS, lc), lambda bi: (bi, 0, 0, 0, 0))],
        out_shape=[jax.ShapeDtypeStruct((b, lc, da_w), BF16),
                   jax.ShapeDtypeStruct((b, heads, 1, HEAD_W + ONES_ROWS, lc), BF16)],
        compiler_params=_cp("parallel"),
        name="ctx_kv_proj",
    )(ctx, sh, sc, w)


def _attn_kernel(qT_ref, kc_ref, vcT_ref, k_ref, vT_ref, lam_ref, g_ref, o_ref,
                 m1_ref, a1_ref, m2_ref, a2_ref, sa1_ref, sa2_ref, sb1_ref, sb2_ref, *, lam_init):
    qT = qT_ref[0]
    row = lax.broadcasted_iota(jnp.int32, qT.shape, 0)
    zero = jnp.zeros_like(qT)
    q1 = jnp.where(row < DA_HEAD_DIM, qT, zero)
    q2 = jnp.where(row >= DA_HEAD_DIM, qT, zero)
    nk = vT_ref.shape[2]

    m1_ref[...] = jnp.full_like(m1_ref, -jnp.inf)
    m2_ref[...] = jnp.full_like(m2_ref, -jnp.inf)
    a1_ref[...] = jnp.zeros_like(a1_ref)
    a2_ref[...] = jnp.zeros_like(a2_ref)

    def softmax_pv(s_ref, rows, vb, m_ref, a_ref):
        m_old = m_ref[...]
        m_new = jnp.maximum(m_old, jnp.max(s_ref[0:rows, :], axis=0, keepdims=True))
        acc = a_ref[...] * jnp.exp2(m_old - m_new)
        step = min(rows, V7X_MXU_DEPTH)
        for lo in range(0, rows, step):
            p = jnp.exp2(s_ref[lo:lo + step, :] - m_new).astype(BF16)
            acc = acc + _dot(vb[:, lo:lo + step], p)
        a_ref[...] = acc
        m_ref[...] = m_new

    def scores(t, s1_ref, s2_ref):
        off = pl.multiple_of(t * ATT_TK, ATT_TK)
        kb = k_ref[0, pl.ds(off, ATT_TK), :]
        s1_ref[...] = _dot(kb, q1)
        s2_ref[...] = _dot(kb, q2)

    def consume(t, s1_ref, s2_ref):
        vb = vT_ref[0, 0, t]
        softmax_pv(s1_ref, ATT_TK, vb, m1_ref, a1_ref)
        softmax_pv(s2_ref, ATT_TK, vb, m2_ref, a2_ref)

    bufs = ((sa1_ref, sa2_ref), (sb1_ref, sb2_ref))

    kc, vc = kc_ref[0], vcT_ref[0, 0, 0]
    lc = kc.shape[0]
    sb1_ref[0:lc, :] = _dot(kc, q1)
    sb2_ref[0:lc, :] = _dot(kc, q2)
    scores(0, sa1_ref, sa2_ref)
    softmax_pv(sb1_ref, lc, vc, m1_ref, a1_ref)
    softmax_pv(sb2_ref, lc, vc, m2_ref, a2_ref)


    unroll = math.gcd(nk, ATT_UNROLL)
    assert unroll % 2 == 0

    def body(j, carry):
        for u in range(unroll):
            t = unroll * j + u
            scores(jnp.minimum(t + 1, nk - 1), *bufs[(u + 1) % 2])
            consume(t, *bufs[u % 2])
        return carry

    lax.fori_loop(0, nk // unroll, body, 0)

    lp = lam_ref[...]
    lam = (jnp.exp(jnp.sum(lp[0:1] * lp[1:2], axis=-1, keepdims=True))
           - jnp.exp(jnp.sum(lp[2:3] * lp[3:4], axis=-1, keepdims=True)) + lam_init)
    a1, a2 = a1_ref[...], a2_ref[...]
    o = (a1[:HEAD_W] * (1.0 / a1[HEAD_W:HEAD_W + 1])
         - lam * (a2[:HEAD_W] * (1.0 / a2[HEAD_W:HEAD_W + 1])))
    ms = jnp.mean(o * o, axis=0, keepdims=True)
    on = (o * lax.rsqrt(ms + LN_EPS)).T
    o_ref[0] = (on * g_ref[...] * (1.0 - lam_init)).astype(o_ref.dtype)


def _diff_attention(qT, k, vT, kc, vcT, lam_rows, subln_g, lam_init):
    b, da_w, l = qT.shape
    lc = kc.shape[1]
    heads = da_w // HEAD_W
    nk = vT.shape[2]
    av = HEAD_W + ONES_ROWS
    tq = _pick(l, (512, 256, 128))
    kmap = lambda bi, h, i: (bi, 0, h)
    vmap = lambda bi, h, i: (bi, h, 0, 0, 0)
    return pl.pallas_call(
        functools.partial(_attn_kernel, lam_init=lam_init),
        grid=(b, heads, l // tq),
        in_specs=[pl.BlockSpec((1, HEAD_W, tq), lambda bi, h, i: (bi, h, i)),
                  pl.BlockSpec((1, lc, HEAD_W), kmap),
                  pl.BlockSpec((1, 1, 1, av, lc), vmap),
                  pl.BlockSpec((1, l, HEAD_W), kmap),
                  pl.BlockSpec((1, 1, nk, av, ATT_TK), vmap),
                  pl.BlockSpec((8, V7X_LANES), lambda bi, h, i: (0, 0)),
                  pl.BlockSpec((1, HEAD_W), lambda bi, h, i: (0, 0))],
        out_specs=pl.BlockSpec((1, tq, HEAD_W), lambda bi, h, i: (bi, i, h)),
        out_shape=jax.ShapeDtypeStruct((b, l, da_w), BF16),
        scratch_shapes=[pltpu.VMEM((1, tq), F32), pltpu.VMEM((av, tq), F32),
                        pltpu.VMEM((1, tq), F32), pltpu.VMEM((av, tq), F32)]
                       + [pltpu.VMEM((ATT_TK, tq), F32)] * 4,
        compiler_params=_cp("parallel", "parallel", "parallel"),
        name="diff_attention",
    )(qT, kc, vcT, k, vT, lam_rows, subln_g)


def _fourier_tables(l, fn_w):
    r = l // DFT_INNER
    gw = fn_w // FN_GROUPS
    cc = np.arange(gw)
    th = 2.0 * np.pi * np.outer(cc, cc) / gw
    eye = np.eye(FN_GROUPS)
    wc = np.kron(eye, np.cos(th) / np.sqrt(gw))
    ws = np.kron(eye, -np.sin(th) / np.sqrt(gw))
    k1 = np.arange(r)[None, :, None]
    n1 = np.arange(r)[None, None, :]
    n2 = np.arange(DFT_INNER)[:, None, None]
    th1 = 2.0 * np.pi * ((k1 * (DFT_INNER * n1 + n2)) % l) / l
    gr, gi = np.cos(th1), -np.sin(th1)
    g = np.concatenate([np.concatenate([gr, -gi], 2), np.concatenate([gi, gr], 2)], 1)
    k2 = np.arange(DFT_INNER)
    th3 = 2.0 * np.pi * np.outer(k2, k2) / DFT_INNER
    m3 = np.concatenate([np.cos(th3), np.sin(th3)], 1) / np.sqrt(l)
    m3 = np.kron(np.eye(2), m3)
    as_bf = lambda a: jnp.asarray(a, dtype=F32).astype(BF16)
    return as_bf(wc), as_bf(ws), as_bf(g), as_bf(m3)


def _swap01(x):
    return jnp.swapaxes(x, 0, 1)


def _fm1_kernel(x_ref, wc_ref, ws_ref, g_ref, o_ref):
    x = _swap01(x_ref[0].astype(F32))
    for r in range(SUB_BLK):
        xr = x[r].astype(BF16)
        zs = jnp.concatenate([_dot(xr, wc_ref[...]), _dot(xr, ws_ref[...])], axis=0).astype(BF16)
        o_ref[0, r] = _dot(g_ref[r], zs).astype(o_ref.dtype)


def _fm2_kernel(ar_ref, ai_ref, m3_ref, o_ref):
    ar = _swap01(ar_ref[0].astype(F32))
    ai = _swap01(ai_ref[0].astype(F32))
    outs = []
    for q in range(SUB_BLK // 2):
        rhs = jnp.concatenate([ar[2 * q], ai[2 * q], ar[2 * q + 1], ai[2 * q + 1]], axis=0).astype(BF16)
        o2 = _dot(m3_ref[...], rhs)
        outs += [o2[:DFT_INNER], o2[DFT_INNER:]]
    o_ref[0] = _swap01(jnp.stack(outs, axis=0)).astype(o_ref.dtype)


def _fourier_mix(f, tables):
    b, l, fn_w = f.shape
    r = l // DFT_INNER
    wc, ws, g, m3 = tables
    const = lambda bi, j: (0, 0)
    a = pl.pallas_call(
        _fm1_kernel,
        grid=(b, DFT_INNER // SUB_BLK),
        in_specs=[pl.BlockSpec((1, r, SUB_BLK, fn_w), lambda bi, j: (bi, 0, j, 0)),
                  pl.BlockSpec((fn_w, fn_w), const), pl.BlockSpec((fn_w, fn_w), const),
                  pl.BlockSpec((SUB_BLK, 2 * r, 2 * r), lambda bi, j: (j, 0, 0))],
        out_specs=pl.BlockSpec((1, SUB_BLK, 2 * r, fn_w), lambda bi, j: (bi, j, 0, 0)),
        out_shape=jax.ShapeDtypeStruct((b, DFT_INNER, 2 * r, fn_w), BF16),
        compiler_params=_cp("parallel", "parallel"),
        name="fourier_stage1",
    )(f.reshape(b, r, DFT_INNER, fn_w), wc, ws, g)
    nkb = r // SUB_BLK
    out = pl.pallas_call(
        _fm2_kernel,
        grid=(b, nkb),
        in_specs=[pl.BlockSpec((1, DFT_INNER, SUB_BLK, fn_w), lambda bi, j: (bi, 0, j, 0)),
                  pl.BlockSpec((1, DFT_INNER, SUB_BLK, fn_w), lambda bi, j: (bi, 0, nkb + j, 0)),
                  pl.BlockSpec((2 * DFT_INNER, 4 * DFT_INNER), const)],
        out_specs=pl.BlockSpec((1, DFT_INNER, SUB_BLK, fn_w), lambda bi, j: (bi, 0, j, 0)),
        out_shape=jax.ShapeDtypeStruct((b, DFT_INNER, r, fn_w), BF16),
        compiler_params=_cp("parallel", "parallel"),
        name="fourier_stage2",
    )(a, a, m3)
    return out.reshape(b, l, fn_w)


def _proj_res_ln_kernel(*refs, n_in, alpha):
    ins = refs[:n_in]
    ws = refs[n_in:2 * n_in]
    x_ref, gate_ref, g_ref, b_ref, o_ref = refs[2 * n_in:]
    y = _dot(ins[0][0].astype(BF16), ws[0][...])
    for a_ref, w_ref in zip(ins[1:], ws[1:]):
        y = y + _dot(a_ref[0].astype(BF16), w_ref[...])
    z = alpha * x_ref[0] + gate_ref[0] * y
    o_ref[0] = _ln(z) * g_ref[...] + b_ref[...]


def _proj_res_ln(acts, weights, x, gate, ln_g, ln_b, alpha, name):
    b, l, d = x.shape
    tm = _pick(l, (512, 256, 128))
    row = lambda bi, i: (bi, i, 0)
    const = lambda bi, i: (0, 0)
    in_specs = ([pl.BlockSpec((1, tm, a.shape[2]), row) for a in acts]
                + [pl.BlockSpec(w.shape, const) for w in weights]
                + [pl.BlockSpec((1, tm, d), row), pl.BlockSpec((1, 1, d), lambda bi, i: (bi, 0, 0)),
                   pl.BlockSpec((1, d), const), pl.BlockSpec((1, d), const)])
    return pl.pallas_call(
        functools.partial(_proj_res_ln_kernel, n_in=len(acts), alpha=alpha),
        grid=(b, l // tm),
        in_specs=in_specs,
        out_specs=pl.BlockSpec((1, tm, d), row),
        out_shape=jax.ShapeDtypeStruct((b, l, d), F32),
        compiler_params=_cp("parallel", "parallel"),
        name=name,
    )(*acts, *weights, x, gate, ln_g.reshape(1, d), ln_b.reshape(1, d))


def _halo_ln(xp_ref, x_ref, xn_ref, sh_ref, sc_ref, first, last):
    mod = lambda v: _ln(v) * (1.0 + sc_ref[0]) + sh_ref[0]
    hp = mod(xp_ref[0]) * jnp.where(first, 0.0, 1.0)
    hn = mod(xn_ref[0]) * jnp.where(last, 0.0, 1.0)
    return jnp.concatenate([hp, mod(x_ref[0]), hn], axis=0).astype(BF16)


def _dwconv3(pre, cw, cb):
    n_rows = pre.shape[0]
    up = pltpu.roll(pre, 1, 0)
    dn = pltpu.roll(pre, n_rows - 1, 0)
    u = up * cw[0:1] + pre * cw[1:2] + dn * cw[2:3] + cb
    return u[HALO:n_rows - HALO]


def _halo_specs(tm, d, l):
    nb = tm // HALO
    last_blk = l // HALO - 1
    return [pl.BlockSpec((1, HALO, d), lambda bi, i, *_: (bi, jnp.maximum(i * nb - 1, 0), 0)),
            pl.BlockSpec((1, tm, d), lambda bi, i, *_: (bi, i, 0)),
            pl.BlockSpec((1, HALO, d), lambda bi, i, *_: (bi, jnp.minimum((i + 1) * nb, last_blk), 0))]


def _erf(x):
    return lax.erf(x)


def _ffn_kernel(xp_ref, x_ref, xn_ref, sh_ref, sc_ref, gate_ref, wi_ref, cw_ref, cb_ref, wo_ref,
                g_ref, b_ref, o_ref, z_ref, *, alpha, f, tf):
    i = pl.program_id(1)
    h = _halo_ln(xp_ref, x_ref, xn_ref, sh_ref, sc_ref, i == 0, i == pl.num_programs(1) - 1)
    for j in range(f // tf):
        a, g = (_dwconv3(_dot(h, wi_ref[:, lo:lo + tf]), cw_ref[:, lo:lo + tf], cb_ref[:, lo:lo + tf])
                for lo in (j * tf, f + j * tf))
        z_ref[:, j * tf:(j + 1) * tf] = (0.5 * a * (1.0 + _erf(a * (2.0 ** -0.5))) * g).astype(BF16)
    y = _dot(z_ref[...], wo_ref[...])
    zz = alpha * x_ref[0] + gate_ref[0] * y
    o_ref[0] = _ln(zz) * g_ref[...] + b_ref[...]


def _conv_ffn(x, sh, sc, gate, w_in, conv_w, conv_b, w_out, ln_g, ln_b, alpha, name):
    b, l, d = x.shape
    f = w_out.shape[0]
    tm = _pick(l, (512, 256, 128))
    tf = _pick(f, (256, 128))
    vec = lambda bi, i: (bi, 0, 0)
    resident = lambda shape: pl.BlockSpec(shape, lambda bi, i: (0, 0), pipeline_mode=pl.Buffered(1))
    return pl.pallas_call(
        functools.partial(_ffn_kernel, alpha=alpha, f=f, tf=tf),
        grid=(b, l // tm),
        in_specs=_halo_specs(tm, d, l) + [
            pl.BlockSpec((1, 1, d), vec), pl.BlockSpec((1, 1, d), vec), pl.BlockSpec((1, 1, d), vec),
            resident((d, 2 * f)), resident((3, 2 * f)), resident((1, 2 * f)), resident((f, d)),
            resident((1, d)), resident((1, d))],
        out_specs=pl.BlockSpec((1, tm, d), lambda bi, i: (bi, i, 0)),
        out_shape=jax.ShapeDtypeStruct((b, l, d), F32),
        scratch_shapes=[pltpu.VMEM((tm, f), BF16)],
        compiler_params=_cp("parallel", "parallel"),
        name=name,
    )(x, x, x, sh, sc, gate, w_in, conv_w, conv_b.reshape(1, 2 * f), w_out,
      ln_g.reshape(1, d), ln_b.reshape(1, d))


def _hy_inproj_kernel(xp_ref, x_ref, xn_ref, sh_ref, sc_ref, w_ref, cw_ref, cb_ref,
                      x0_ref, vx_ref, *, d, tn):
    i = pl.program_id(1)
    h = _halo_ln(xp_ref, x_ref, xn_ref, sh_ref, sc_ref, i == 0, i == pl.num_programs(1) - 1)
    for c in range(d // tn):
        def part(k):
            lo = k * d + c * tn
            return _dwconv3(_dot(h, w_ref[:, lo:lo + tn]), cw_ref[:, lo:lo + tn], cb_ref[:, lo:lo + tn])
        x0_ref[0, :, c * tn:(c + 1) * tn] = part(0)
        vx_ref[0, :, c * tn:(c + 1) * tn] = part(2) * part(1)


def _hy_inproj(x, sh, sc, w, conv_w, conv_b):
    b, l, d = x.shape
    tm = _pick(l, (512, 256, 128))
    tn = _pick(d, (256, 128))
    vec = lambda bi, i: (bi, 0, 0)
    const = lambda bi, i: (0, 0)
    return pl.pallas_call(
        functools.partial(_hy_inproj_kernel, d=d, tn=tn),
        grid=(b, l // tm),
        in_specs=_halo_specs(tm, d, l) + [
            pl.BlockSpec((1, 1, d), vec), pl.BlockSpec((1, 1, d), vec),
            pl.BlockSpec((d, 3 * d), const), pl.BlockSpec((3, 3 * d), const),
            pl.BlockSpec((1, 3 * d), const)],
        out_specs=[pl.BlockSpec((1, tm, d), lambda bi, i: (bi, i, 0)),
                   pl.BlockSpec((1, tm, d), lambda bi, i: (bi, i, 0))],
        out_shape=[jax.ShapeDtypeStruct((b, l, d), F32), jax.ShapeDtypeStruct((b, l, d), F32)],
        compiler_params=_cp("parallel", "parallel"),
        name="hyena_inproj",
    )(x, x, x, sh, sc, w, conv_w, conv_b.reshape(1, 3 * d))


def _hy_filter_kernel(z_ref, w1_ref, b1_ref, w2_ref, b2_ref, w3_ref, b3_ref, fr_ref, w4_ref,
                      t_ref, dl_ref, hfb_ref, inv_ref, *, d):
    i = pl.program_id(0)
    fr = fr_ref[...]
    hdn = jnp.sin(fr * (_dot3(z_ref[...], w1_ref[...]) + b1_ref[...]))
    hdn = jnp.sin(fr * (_dot3(hdn, w2_ref[...]) + b2_ref[...]))
    hdn = jnp.sin(fr * (_dot3(hdn, w3_ref[...]) + b3_ref[...]))
    h = _dot3(hdn, w4_ref[...])
    decay = jnp.exp(-t_ref[...] * dl_ref[...])
    hf = h[:, :d] * decay
    hb = h[:, d:] * decay
    row = lax.broadcasted_iota(jnp.int32, hb.shape, 0)
    hb = jnp.where((row == 0) & (i == 0), 0.0, hb)
    hfb_ref[0] = hf
    hfb_ref[1] = hb

    @pl.when(i == 0)
    def _():
        inv_ref[...] = jnp.zeros_like(inv_ref)

    inv_ref[...] += (jnp.sum(jnp.abs(hf), axis=0, keepdims=True)
                     + jnp.sum(jnp.abs(hb), axis=0, keepdims=True))

    @pl.when(i == pl.num_programs(0) - 1)
    def _():
        inv_ref[...] = 1.0 / inv_ref[...]


def _hy_filters(l, d, w1, b1, w2, b2, w3, b3, freq, w4):
    hh = w1.shape[1]
    bands = (HY_FILTER_EMB - 1) // 2
    t = np.linspace(0.0, 1.0, l, dtype=np.float32)[:, None]
    w = (2.0 * math.pi * np.arange(l, dtype=np.float32)[:, None] / l).astype(np.float32)
    fb = np.linspace(1e-4, bands - 1, bands, dtype=np.float32)[None, :]
    z = np.concatenate([t, np.cos(fb * w), -np.sin(fb * w)], axis=-1).astype(np.float32)
    zp = np.zeros((l, V7X_LANES), np.float32)
    zp[:, :HY_FILTER_EMB] = z
    w1p = jnp.zeros((V7X_LANES, hh), F32).at[:HY_FILTER_EMB].set(w1.astype(F32))
    min_decay = math.log(HY_DECAY_TARGET) / HY_SLOW_DECAY
    max_decay = math.log(HY_DECAY_TARGET) / HY_FAST_DECAY
    deltas = np.abs(np.linspace(min_decay, max_decay, d, dtype=np.float32))[None, :]
    tl = _pick(l, (512, 256, 128))
    const = lambda i: (0, 0)
    r2 = lambda a: a.astype(F32).reshape(1, -1)
    return pl.pallas_call(
        functools.partial(_hy_filter_kernel, d=d),
        grid=(l // tl,),
        in_specs=[pl.BlockSpec((tl, V7X_LANES), lambda i: (i, 0)),
                  pl.BlockSpec((V7X_LANES, hh), const), pl.BlockSpec((1, hh), const),
                  pl.BlockSpec((hh, hh), const), pl.BlockSpec((1, hh), const),
                  pl.BlockSpec((hh, hh), const), pl.BlockSpec((1, hh), const),
                  pl.BlockSpec((1, hh), const), pl.BlockSpec((hh, 2 * d), const),
                  pl.BlockSpec((tl, 1), lambda i: (i, 0)), pl.BlockSpec((1, d), const)],
        out_specs=[pl.BlockSpec((2, tl, d), lambda i: (0, i, 0)), pl.BlockSpec((1, d), const)],
        out_shape=[jax.ShapeDtypeStruct((2, l, d), F32), jax.ShapeDtypeStruct((1, d), F32)],
        compiler_params=_cp("arbitrary"),
        name="hyena_filter",
    )(jnp.asarray(zp), w1p, r2(b1), w2.astype(F32), r2(b2), w3.astype(F32), r2(b3), r2(freq),
      w4.astype(F32), jnp.asarray(t), jnp.asarray(deltas))


def _hyena_tables(l):
    r = l // DFT_INNER
    n = 2 * l
    k1 = np.arange(2 * r)[None, :, None]
    n1 = np.arange(r)[None, None, :]
    n2 = np.arange(DFT_INNER)[:, None, None]
    th = 2.0 * np.pi * ((k1 * (DFT_INNER * n1 + n2)) % n) / n
    gr, gi = np.cos(th), -np.sin(th)
    g1r = np.concatenate([gr, gi], 1)
    g1c = np.concatenate([np.concatenate([gr, -gi], 2), np.concatenate([gi, gr], 2)], 1)
    k2 = np.arange(DFT_INNER)
    th3 = 2.0 * np.pi * np.outer(k2, k2) / DFT_INNER
    c, s = np.cos(th3), np.sin(th3)
    m3f = np.kron(np.eye(2), np.block([[c, s], [-s, c]]))
    m3i = np.kron(np.eye(2), np.block([[c, -s], [s, c]]))
    thi = np.transpose(th, (0, 2, 1))
    hr, hi = np.cos(thi), np.sin(thi)
    hc = np.concatenate([np.concatenate([hr, -hi], 2), np.concatenate([hi, hr], 2)], 1) / n
    as_bf = lambda a: jnp.asarray(a, dtype=F32).astype(BF16)
    return as_bf(g1r), as_bf(g1c), as_bf(m3f), as_bf(m3i), as_bf(hc)


def _hy1_kernel(v_ref, g_ref, ar_ref, ai_ref, *, r2):
    xa = _swap01(v_ref[0, 0])
    xb = _swap01(v_ref[0, 1])
    for s in range(SUB_BLK_F32):
        z = jnp.concatenate([xa[s], xb[s]], axis=0).astype(BF16)
        a = _dot(g_ref[s], z)
        ar_ref[0, s] = a[:r2].astype(BF16)
        ai_ref[0, s] = a[r2:].astype(BF16)


def _hy1f_kernel(v_ref, g_ref, ar_ref, ai_ref, *, r2):
    x = _swap01(v_ref[0])
    for s in range(SUB_BLK_F32):
        a = _dot(g_ref[s], x[s].astype(BF16))
        ar_ref[0, s] = a[:r2].astype(BF16)
        ai_ref[0, s] = a[r2:].astype(BF16)


def _hy_stage1(v5, g, td, real_input):
    r, d = v5.shape[-3], v5.shape[-1]
    nseq = v5.shape[0]
    nb = DFT_INNER // SUB_BLK_F32
    if real_input:
        vspec = pl.BlockSpec((1, r, SUB_BLK_F32, td), lambda p, j, h: (p, 0, j, h))
        body = _hy1f_kernel
    else:
        vspec = pl.BlockSpec((1, 2, r, SUB_BLK_F32, td), lambda p, j, h: (p, 0, 0, j, h))
        body = _hy1_kernel
    ospec = pl.BlockSpec((1, SUB_BLK_F32, 2 * r, td), lambda p, j, h: (p, j, 0, h))
    oshape = jax.ShapeDtypeStruct((nseq, DFT_INNER, 2 * r, d), BF16)
    return pl.pallas_call(
        functools.partial(body, r2=2 * r),
        grid=(nseq, nb, d // td),
        in_specs=[vspec, pl.BlockSpec((SUB_BLK_F32,) + g.shape[1:], lambda p, j, h: (j, 0, 0))],
        out_specs=[ospec, ospec],
        out_shape=[oshape, oshape],
        compiler_params=_cp("parallel", "parallel", "parallel"),
        name="hyena_fft_stage1_filter" if real_input else "hyena_fft_stage1",
    )(v5, g)


def _k1_pair_rhs(fr, fi, q):
    return jnp.concatenate([fr[2 * q], fi[2 * q], fr[2 * q + 1], fi[2 * q + 1]], axis=0).astype(BF16)


def _hy2f_kernel(fr_ref, fi_ref, inv_ref, m3f_ref, kr_ref, ki_ref):
    h = DFT_INNER
    frh, fih = _swap01(fr_ref[0].astype(F32)), _swap01(fi_ref[0].astype(F32))
    frb, fib = _swap01(fr_ref[1].astype(F32)), _swap01(fi_ref[1].astype(F32))
    inv = inv_ref[...]
    for q in range(SUB_BLK // 2):
        sh = _dot(m3f_ref[...], _k1_pair_rhs(frh, fih, q))
        sb = _dot(m3f_ref[...], _k1_pair_rhs(frb, fib, q))
        for u in range(2):
            o = 2 * h * u
            kr_ref[2 * q + u] = ((sh[o:o + h] + sb[o:o + h]) * inv).astype(BF16)
            ki_ref[2 * q + u] = ((sh[o + h:o + 2 * h] - sb[o + h:o + 2 * h]) * inv).astype(BF16)


def _hy_filter_spectrum(afr, afi, inv_norm, m3f, td):
    _, _, r2, d = afr.shape
    fspec = pl.BlockSpec((2, DFT_INNER, SUB_BLK, td), lambda j, h: (0, 0, j, h))
    kspec = pl.BlockSpec((SUB_BLK, DFT_INNER, td), lambda j, h: (j, 0, h))
    kshape = jax.ShapeDtypeStruct((r2, DFT_INNER, d), BF16)
    return pl.pallas_call(
        _hy2f_kernel,
        grid=(r2 // SUB_BLK, d // td),
        in_specs=[fspec, fspec, pl.BlockSpec((1, td), lambda j, h: (0, h)),
                  pl.BlockSpec(m3f.shape, lambda j, h: (0, 0))],
        out_specs=[kspec, kspec],
        out_shape=[kshape, kshape],
        compiler_params=_cp("parallel", "parallel"),
        name="hyena_filter_spectrum",
    )(afr, afi, inv_norm, m3f)


def _hy2_kernel(ar_ref, ai_ref, kr_ref, ki_ref, m3f_ref, m3i_ref, br_ref, bi_ref):
    h = DFT_INNER
    ar, ai = _swap01(ar_ref[0].astype(F32)), _swap01(ai_ref[0].astype(F32))
    brs, bis = [], []
    for q in range(SUB_BLK // 2):
        v = _dot(m3f_ref[...], _k1_pair_rhs(ar, ai, q))
        ys = []
        for u in range(2):
            o = 2 * h * u
            vr, vi = v[o:o + h], v[o + h:o + 2 * h]
            kr, ki = kr_ref[2 * q + u].astype(F32), ki_ref[2 * q + u].astype(F32)
            ys += [vr * kr - vi * ki, vr * ki + vi * kr]
        bn = _dot(m3i_ref[...], jnp.concatenate(ys, axis=0).astype(BF16))
        brs += [bn[0:h], bn[2 * h:3 * h]]
        bis += [bn[h:2 * h], bn[3 * h:4 * h]]
    br_ref[0] = _swap01(jnp.stack(brs, axis=0)).astype(BF16)
    bi_ref[0] = _swap01(jnp.stack(bis, axis=0)).astype(BF16)


def _hy_stage2(ar, ai, kr, ki, m3f, m3i, td):
    p, _, r2, d = ar.shape
    aspec = pl.BlockSpec((1, DFT_INNER, SUB_BLK, td), lambda j, h, pi: (pi, 0, j, h))
    kspec = pl.BlockSpec((SUB_BLK, DFT_INNER, td), lambda j, h, pi: (j, 0, h))
    mspec = pl.BlockSpec(m3f.shape, lambda j, h, pi: (0, 0))
    oshape = jax.ShapeDtypeStruct(ar.shape, BF16)
    return pl.pallas_call(
        _hy2_kernel,
        grid=(r2 // SUB_BLK, d // td, p),
        in_specs=[aspec, aspec, kspec, kspec, mspec, mspec],
        out_specs=[aspec, aspec],
        out_shape=[oshape, oshape],
        compiler_params=_cp("parallel", "parallel", "parallel"),
        name="hyena_fft_stage2",
    )(ar, ai, kr, ki, m3f, m3i)


def _hy3_kernel(br_ref, bi_ref, hc_ref, vx_ref, x0_ref, d_ref, o_ref, *, r):
    ya, yb = [], []
    for s in range(SUB_BLK_F32):
        bb = jnp.concatenate([br_ref[0, s], bi_ref[0, s]], axis=0)
        y = _dot(hc_ref[s], bb)
        ya.append(y[:r])
        yb.append(y[r:])
    for u, ys in enumerate((ya, yb)):
        y = _swap01(jnp.stack(ys, axis=0))
        o_ref[0, u] = (y + vx_ref[0, u] * d_ref[...]) * x0_ref[0, u]


def _hy_stage3(br, bi, hc, vx5, x05, d_skip, td):
    p, _, r, _, d = vx5.shape
    bspec = pl.BlockSpec((1, SUB_BLK_F32, 2 * r, td), lambda pi, j, h: (pi, j, 0, h))
    vspec = pl.BlockSpec((1, 2, r, SUB_BLK_F32, td), lambda pi, j, h: (pi, 0, 0, j, h))
    return pl.pallas_call(
        functools.partial(_hy3_kernel, r=r),
        grid=(p, DFT_INNER // SUB_BLK_F32, d // td),
        in_specs=[bspec, bspec,
                  pl.BlockSpec((SUB_BLK_F32, 2 * r, 4 * r), lambda pi, j, h: (j, 0, 0)),
                  vspec, vspec, pl.BlockSpec((1, 1, td), lambda pi, j, h: (0, 0, h))],
        out_specs=vspec,
        out_shape=jax.ShapeDtypeStruct(vx5.shape, F32),
        compiler_params=_cp("parallel", "parallel", "parallel"),
        name="hyena_fft_stage3",
    )(br, bi, hc, vx5, x05, d_skip.astype(F32).reshape(1, 1, d))


def _hyena_long_conv(vx, x0, hfb, inv_norm, d_skip, tables):
    b, l, d = vx.shape
    r = l // DFT_INNER
    g1r, g1c, m3f, m3i, hc = tables
    td = _pick(d, (512, 256, 128))
    pair = lambda a: a.reshape(b // 2, 2, r, DFT_INNER, d)
    ar, ai = _hy_stage1(pair(vx), g1c, td, real_input=False)
    afr, afi = _hy_stage1(hfb.reshape(2, r, DFT_INNER, d), g1r, td, real_input=True)
    kr, ki = _hy_filter_spectrum(afr, afi, inv_norm, m3f, td)
    br, bi = _hy_stage2(ar, ai, kr, ki, m3f, m3i, td)
    return _hy_stage3(br, bi, hc, pair(vx), pair(x0), d_skip, td).reshape(b, l, d)


def _rope_tables(l):
    rows = l // GRID_W
    axis_dim = DA_HEAD_DIM // 2
    pos_r = jnp.repeat(jnp.arange(rows, dtype=F32), GRID_W)
    pos_c = jnp.tile(jnp.arange(GRID_W, dtype=F32), rows)
    inv = ROPE_THETA ** (-jnp.arange(0, axis_dim, 2, dtype=F32) / axis_dim)
    ar, ac = pos_r[:, None] * inv, pos_c[:, None] * inv
    n = ar.shape[1]
    zeros = jnp.zeros((l, n), F32)
    cos64 = jnp.concatenate([jnp.cos(ar), jnp.cos(ar), jnp.cos(ac), jnp.cos(ac)], -1)
    sa64 = jnp.concatenate([-jnp.sin(ar), zeros, -jnp.sin(ac), zeros], -1)
    sb64 = jnp.concatenate([zeros, jnp.sin(ar), zeros, jnp.sin(ac)], -1)
    rep = V7X_LANES // DA_HEAD_DIM
    return jnp.tile(cos64, (1, rep)), jnp.tile(sa64, (1, rep)), jnp.tile(sb64, (1, rep))


def _lambda_init(layer_idx):
    return 0.8 - 0.6 * math.exp(-0.3 * layer_idx)


def kernel(x, c, ctx, c_ctx, mod_w, mod_b, ln1_g, ln1_b, ln2_g, ln2_b, ffn_w_in, ffn_conv_w, ffn_conv_b, ffn_w_out, da_w_in, da_w_out, da_lam_q1, da_lam_k1, da_lam_q2, da_lam_k2, da_subln_g, hy_w_in, hy_conv_w, hy_conv_b, hy_f_w1, hy_f_b1, hy_f_w2, hy_f_b2, hy_f_w3, hy_f_b3, hy_f_freq, hy_f_w4, hy_d, hy_w_out):
    b, l, d = x.shape
    depth = mod_w.shape[0]
    assert depth == 2 and l % ATT_TK == 0 and d % (2 * V7X_LANES) == 0
    assert b % 2 == 0 and (l // DFT_INNER) % SUB_BLK == 0
    da_w = d // 2
    alpha = (2 * depth) ** 0.25

    rows = -(-(b + 1) // 8) * 8
    cv = jnp.zeros((rows, d), F32).at[:b].set(c.astype(F32)).at[b].set(c_ctx.astype(F32))
    mod = _mod_params(cv, mod_w.astype(F32), mod_b.astype(F32))

    def mods(i):
        m = mod[i].reshape(rows, N_MOD, d)
        return [m[:b, k][:, None, :] for k in range(N_MOD)], [m[b:b + 1, k][:, None, :] for k in range(N_MOD)]

    (sh1, sc1, g1, sh2, sc2, g2), (csh1, csc1, _, _, _, _) = mods(0)
    w_in = da_w_in[0].astype(BF16)
    cos, sa, sb = _rope_tables(l)
    qT, k, vT, f = _inproj(x, sh1, sc1, w_in, cos, sa, sb, da_w)
    kc, vcT = _ctx_kv(ctx, csh1, csc1, w_in[:, da_w:3 * da_w], da_w)
    lam_rows = jnp.zeros((8, V7X_LANES), F32)
    for r_i, p in enumerate((da_lam_q1[0], da_lam_k1[0], da_lam_q2[0], da_lam_k2[0])):
        lam_rows = lam_rows.at[r_i, :DA_HEAD_DIM].set(p.astype(F32))
    o = _diff_attention(qT, k, vT, kc, vcT, lam_rows, da_subln_g[0].astype(F32).reshape(1, -1), _lambda_init(0))
    fm = _fourier_mix(f, _fourier_tables(l, d - da_w))
    w_out = da_w_out[0].astype(BF16)
    x = _proj_res_ln([o, fm], [w_out[:da_w], w_out[da_w:]], x, g1, ln1_g[0], ln1_b[0], alpha, "l0_outproj")
    x = _conv_ffn(x, sh2, sc2, g2, ffn_w_in[0].astype(BF16), ffn_conv_w[0].astype(F32),
                  ffn_conv_b[0].astype(F32), ffn_w_out[0].astype(BF16), ln2_g[0], ln2_b[0], alpha, "l0_ffn")

    (sh1, sc1, g1, sh2, sc2, g2), _ = mods(1)
    x0, vx = _hy_inproj(x, sh1, sc1, hy_w_in[0].astype(BF16), hy_conv_w[0].astype(F32),
                        hy_conv_b[0].astype(F32))
    hfb, inv_norm = _hy_filters(l, d, hy_f_w1[0], hy_f_b1[0], hy_f_w2[0], hy_f_b2[0],
                                hy_f_w3[0], hy_f_b3[0], hy_f_freq[0], hy_f_w4[0])
    z = _hyena_long_conv(vx, x0, hfb, inv_norm, hy_d[0], _hyena_tables(l))
    x = _proj_res_ln([z], [hy_w_out[0].astype(BF16)], x, g1, ln1_g[1], ln1_b[1], alpha, "l1_outproj")
    x = _conv_ffn(x, sh2, sc2, g2, ffn_w_in[1].astype(BF16), ffn_conv_w[1].astype(F32),
                  ffn_conv_b[1].astype(F32), ffn_w_out[1].astype(BF16), ln2_g[1], ln2_b[1], alpha, "l1_ffn")
    return x
```

```python
import functools
import math

import numpy as np
import jax
import jax.numpy as jnp
from jax import lax
from jax.experimental import pallas as pl
from jax.experimental.pallas import tpu as pltpu

F32 = jnp.float32
BF16 = jnp.bfloat16

GRID_W = 64
DA_HEAD_DIM = 64
ROPE_THETA = 10000.0
N_MOD = 6
LN_EPS = 1e-5
FN_GROUPS = 4
HY_FILTER_EMB = 33
HY_DECAY_TARGET = 1e-2
HY_FAST_DECAY = 0.3
HY_SLOW_DECAY = 1.5
DFT_INNER = GRID_W

V7X_LANES = 128
V7X_MXU_DEPTH = 256
V7X_VMEM_LIMIT_BYTES = 56 * 1024 * 1024
HALO = 8
ACT_HALO = 16
HEAD_W = 2 * DA_HEAD_DIM
ONES_ROWS = 16
ATT_TK = 512
ATT_UNROLL = 8
SUB_BLK = 16
SUB_BLK_F32 = 8


def _cp(*sem):
    return pltpu.CompilerParams(dimension_semantics=sem, vmem_limit_bytes=V7X_VMEM_LIMIT_BYTES)


def _dot(a, b):
    return jnp.dot(a, b, preferred_element_type=F32)


def _split(a):
    hi = a.astype(BF16)
    lo = (a - hi.astype(F32)).astype(BF16)
    return hi, lo


def _dot3(a, b):
    ah, al = _split(a)
    bh, bl = _split(b)
    return _dot(ah, bh) + _dot(al, bh) + _dot(ah, bl)


def _ln(x):
    mu = jnp.mean(x, axis=-1, keepdims=True)
    xc = x - mu
    var = jnp.mean(xc * xc, axis=-1, keepdims=True)
    return xc * lax.rsqrt(var + LN_EPS)


def _pick(n, prefs):
    for p in prefs:
        if n % p == 0:
            return p
    return n


def _mod_kernel(cv_ref, w_ref, b_ref, o_ref):
    cv = cv_ref[...]
    s = cv * jax.nn.sigmoid(cv)
    o_ref[0] = _dot3(s, w_ref[0]) + b_ref[0]


def _mod_params(cv, mod_w, mod_b):
    depth, d, n = mod_w.shape
    rows = cv.shape[0]
    tn = _pick(n, (1536, 1024, 512, 256, 128))
    return pl.pallas_call(
        _mod_kernel,
        grid=(depth, n // tn),
        in_specs=[pl.BlockSpec((rows, d), lambda i, j: (0, 0)),
                  pl.BlockSpec((1, d, tn), lambda i, j: (i, 0, j)),
                  pl.BlockSpec((1, 1, tn), lambda i, j: (i, 0, j))],
        out_specs=pl.BlockSpec((1, rows, tn), lambda i, j: (i, 0, j)),
        out_shape=jax.ShapeDtypeStruct((depth, rows, n), F32),
        compiler_params=_cp("parallel", "parallel"),
        name="mod_params",
    )(cv, mod_w, mod_b.reshape(depth, 1, n))


def _rope_block(blk, cos, sin_a, sin_b):
    return (blk * cos + pltpu.roll(blk, V7X_LANES - 16, 1) * sin_a + pltpu.roll(blk, 16, 1) * sin_b)


def _store_vT(v, vT_ref, head):
    vT_ref[0, head, 0, 0:HEAD_W, :] = v.T.astype(BF16)
    vT_ref[0, head, 0, HEAD_W:, :] = jnp.ones((ONES_ROWS, v.shape[0]), BF16)


def _inproj_kernel(x_ref, sh_ref, sc_ref, w_ref, cos_ref, sa_ref, sb_ref,
                   qT_ref, k_ref, vT_ref, f_ref, *, da_w, q_scale):
    h = (_ln(x_ref[0]) * (1.0 + sc_ref[0]) + sh_ref[0]).astype(BF16)
    p = _dot(h, w_ref[...])
    cos, sa, sb = cos_ref[...], sa_ref[...], sb_ref[...]
    for j in range(da_w // HEAD_W):
        lo, hi = j * HEAD_W, (j + 1) * HEAD_W
        qT_ref[0, lo:hi, :] = (_rope_block(p[:, lo:hi], cos, sa, sb) * q_scale).T.astype(BF16)
        k_ref[0, :, lo:hi] = _rope_block(p[:, da_w + lo:da_w + hi], cos, sa, sb).astype(BF16)
        _store_vT(p[:, 2 * da_w + lo:2 * da_w + hi], vT_ref, j)
    f_ref[0] = p[:, 3 * da_w:].astype(BF16)


def _inproj(x, sh, sc, w, cos, sa, sb, da_w):
    b, l, d = x.shape
    n = w.shape[1]
    fn_w = n - 3 * da_w
    heads = da_w // HEAD_W
    tm = ATT_TK
    row = lambda bi, i: (bi, i, 0)
    vec = lambda bi, i: (bi, 0, 0)
    tab = lambda bi, i: (i, 0)
    return pl.pallas_call(
        functools.partial(_inproj_kernel, da_w=da_w, q_scale=DA_HEAD_DIM ** -0.5 * math.log2(math.e)),
        grid=(b, l // tm),
        in_specs=[pl.BlockSpec((1, tm, d), row),
                  pl.BlockSpec((1, 1, d), vec), pl.BlockSpec((1, 1, d), vec),
                  pl.BlockSpec((d, n), lambda bi, i: (0, 0)),
                  pl.BlockSpec((tm, V7X_LANES), tab), pl.BlockSpec((tm, V7X_LANES), tab),
                  pl.BlockSpec((tm, V7X_LANES), tab)],
        out_specs=[pl.BlockSpec((1, da_w, tm), lambda bi, i: (bi, 0, i)),
                   pl.BlockSpec((1, tm, da_w), row),
                   pl.BlockSpec((1, heads, 1, HEAD_W + ONES_ROWS, tm), lambda bi, i: (bi, 0, i, 0, 0)),
                   pl.BlockSpec((1, tm, fn_w), row)],
        out_shape=[jax.ShapeDtypeStruct((b, da_w, l), BF16), jax.ShapeDtypeStruct((b, l, da_w), BF16),
                   jax.ShapeDtypeStruct((b, heads, l // tm, HEAD_W + ONES_ROWS, tm), BF16),
                   jax.ShapeDtypeStruct((b, l, fn_w), BF16)],
        compiler_params=_cp("parallel", "parallel"),
        name="l0_inproj",
    )(x, sh, sc, w, cos, sa, sb)


def _ctx_kv_kernel(x_ref, sh_ref, sc_ref, w_ref, k_ref, vT_ref, *, da_w):
    h = (_ln(x_ref[0]) * (1.0 + sc_ref[0]) + sh_ref[0]).astype(BF16)
    p = _dot(h, w_ref[...])
    k_ref[0] = p[:, :da_w].astype(BF16)
    for j in range(da_w // HEAD_W):
        _store_vT(p[:, da_w + j * HEAD_W:da_w + (j + 1) * HEAD_W], vT_ref, j)


def _ctx_kv(ctx, sh, sc, w, da_w):
    b, lc, d = ctx.shape
    heads = da_w // HEAD_W
    return pl.pallas_call(
        functools.partial(_ctx_kv_kernel, da_w=da_w),
        grid=(b,),
        in_specs=[pl.BlockSpec((1, lc, d), lambda bi: (bi, 0, 0)),
                  pl.BlockSpec((1, 1, d), lambda bi: (0, 0, 0)),
                  pl.BlockSpec((1, 1, d), lambda bi: (0, 0, 0)),
                  pl.BlockSpec((d, 2 * da_w), lambda bi: (0, 0))],
        out_specs=[pl.BlockSpec((1, lc, da_w), lambda bi: (bi, 0, 0)),
                   pl.BlockSpec((1, heads, 1, HEAD_W + ONES_ROWS, lc), lambda bi: (bi, 0, 0, 0, 0))],
        out_shape=[jax.ShapeDtypeStruct((b, lc, da_w), BF16),
                   jax.ShapeDtypeStruct((b, heads, 1, HEAD_W + ONES_ROWS, lc), BF16)],
        compiler_params=_cp("parallel"),
        name="ctx_kv_proj",
    )(ctx, sh, sc, w)


def _attn_kernel(qT_ref, kc_ref, vcT_ref, k_ref, vT_ref, lam_ref, g_ref, o_ref,
                 m1_ref, a1_ref, m2_ref, a2_ref, sa1_ref, sa2_ref, sb1_ref, sb2_ref, *, lam_init):
    qT = qT_ref[0]
    row = lax.broadcasted_iota(jnp.int32, qT.shape, 0)
    zero = jnp.zeros_like(qT)
    q1 = jnp.where(row < DA_HEAD_DIM, qT, zero)
    q2 = jnp.where(row >= DA_HEAD_DIM, qT, zero)
    nk = vT_ref.shape[2]

    m1_ref[...] = jnp.full_like(m1_ref, -jnp.inf)
    m2_ref[...] = jnp.full_like(m2_ref, -jnp.inf)
    a1_ref[...] = jnp.zeros_like(a1_ref)
    a2_ref[...] = jnp.zeros_like(a2_ref)

    def softmax_pv(s_ref, rows, vb, m_ref, a_ref):
        m_old = m_ref[...]
        m_new = jnp.maximum(m_old, jnp.max(s_ref[0:rows, :], axis=0, keepdims=True))
        acc = a_ref[...] * jnp.exp2(m_old - m_new)
        step = min(rows, V7X_MXU_DEPTH)
        for lo in range(0, rows, step):
            p = jnp.exp2(s_ref[lo:lo + step, :] - m_new).astype(BF16)
            acc = acc + _dot(vb[:, lo:lo + step], p)
        a_ref[...] = acc
        m_ref[...] = m_new

    def scores(t, s1_ref, s2_ref):
        off = pl.multiple_of(t * ATT_TK, ATT_TK)
        kb = k_ref[0, pl.ds(off, ATT_TK), :]
        s1_ref[...] = _dot(kb, q1)
        s2_ref[...] = _dot(kb, q2)

    def consume(t, s1_ref, s2_ref):
        vb = vT_ref[0, 0, t]
        softmax_pv(s1_ref, ATT_TK, vb, m1_ref, a1_ref)
        softmax_pv(s2_ref, ATT_TK, vb, m2_ref, a2_ref)

    bufs = ((sa1_ref, sa2_ref), (sb1_ref, sb2_ref))

    kc, vc = kc_ref[0], vcT_ref[0, 0, 0]
    lc = kc.shape[0]
    sb1_ref[0:lc, :] = _dot(kc, q1)
    sb2_ref[0:lc, :] = _dot(kc, q2)
    scores(0, sa1_ref, sa2_ref)
    softmax_pv(sb1_ref, lc, vc, m1_ref, a1_ref)
    softmax_pv(sb2_ref, lc, vc, m2_ref, a2_ref)


    unroll = math.gcd(nk, ATT_UNROLL)
    assert unroll % 2 == 0

    def body(j, carry):
        for u in range(unroll):
            t = unroll * j + u
            scores(jnp.minimum(t + 1, nk - 1), *bufs[(u + 1) % 2])
            consume(t, *bufs[u % 2])
        return carry

    lax.fori_loop(0, nk // unroll, body, 0)

    lp = lam_ref[...]
    lam = (jnp.exp(jnp.sum(lp[0:1] * lp[1:2], axis=-1, keepdims=True))
           - jnp.exp(jnp.sum(lp[2:3] * lp[3:4], axis=-1, keepdims=True)) + lam_init)
    a1, a2 = a1_ref[...], a2_ref[...]
    o = (a1[:HEAD_W] * (1.0 / a1[HEAD_W:HEAD_W + 1])
         - lam * (a2[:HEAD_W] * (1.0 / a2[HEAD_W:HEAD_W + 1])))
    ms = jnp.mean(o * o, axis=0, keepdims=True)
    on = (o * lax.rsqrt(ms + LN_EPS)).T
    o_ref[0] = (on * g_ref[...] * (1.0 - lam_init)).astype(o_ref.dtype)


def _diff_attention(qT, k, vT, kc, vcT, lam_rows, subln_g, lam_init):
    b, da_w, l = qT.shape
    lc = kc.shape[1]
    heads = da_w // HEAD_W
    nk = vT.shape[2]
    av = HEAD_W + ONES_ROWS
    tq = _pick(l, (512, 256, 128))
    kmap = lambda bi, h, i: (bi, 0, h)
    vmap = lambda bi, h, i: (bi, h, 0, 0, 0)
    return pl.pallas_call(
        functools.partial(_attn_kernel, lam_init=lam_init),
        grid=(b, heads, l // tq),
        in_specs=[pl.BlockSpec((1, HEAD_W, tq), lambda bi, h, i: (bi, h, i)),
                  pl.BlockSpec((1, lc, HEAD_W), kmap),
                  pl.BlockSpec((1, 1, 1, av, lc), vmap),
                  pl.BlockSpec((1, l, HEAD_W), kmap),
                  pl.BlockSpec((1, 1, nk, av, ATT_TK), vmap),
                  pl.BlockSpec((8, V7X_LANES), lambda bi, h, i: (0, 0)),
                  pl.BlockSpec((1, HEAD_W), lambda bi, h, i: (0, 0))],
        out_specs=pl.BlockSpec((1, tq, HEAD_W), lambda bi, h, i: (bi, i, h)),
        out_shape=jax.ShapeDtypeStruct((b, l, da_w), BF16),
        scratch_shapes=[pltpu.VMEM((1, tq), F32), pltpu.VMEM((av, tq), F32),
                        pltpu.VMEM((1, tq), F32), pltpu.VMEM((av, tq), F32)]
                       + [pltpu.VMEM((ATT_TK, tq), F32)] * 4,
        compiler_params=_cp("parallel", "parallel", "parallel"),
        name="diff_attention",
    )(qT, kc, vcT, k, vT, lam_rows, subln_g)


def _fourier_tables(l, fn_w):
    r = l // DFT_INNER
    gw = fn_w // FN_GROUPS
    cc = np.arange(gw)
    th = 2.0 * np.pi * np.outer(cc, cc) / gw
    eye = np.eye(FN_GROUPS)
    wc = np.kron(eye, np.cos(th) / np.sqrt(gw))
    ws = np.kron(eye, -np.sin(th) / np.sqrt(gw))
    k1 = np.arange(r)[None, :, None]
    n1 = np.arange(r)[None, None, :]
    n2 = np.arange(DFT_INNER)[:, None, None]
    th1 = 2.0 * np.pi * ((k1 * (DFT_INNER * n1 + n2)) % l) / l
    gr, gi = np.cos(th1), -np.sin(th1)
    g = np.concatenate([np.concatenate([gr, -gi], 2), np.concatenate([gi, gr], 2)], 1)
    k2 = np.arange(DFT_INNER)
    th3 = 2.0 * np.pi * np.outer(k2, k2) / DFT_INNER
    m3 = np.concatenate([np.cos(th3), np.sin(th3)], 1) / np.sqrt(l)
    m3 = np.kron(np.eye(2), m3)
    as_bf = lambda a: jnp.asarray(a, dtype=F32).astype(BF16)
    return as_bf(wc), as_bf(ws), as_bf(g), as_bf(m3)


def _swap01(x):
    return jnp.swapaxes(x, 0, 1)


def _fm1_kernel(x_ref, wc_ref, ws_ref, g_ref, o_ref):
    x = _swap01(x_ref[0].astype(F32))
    for r in range(SUB_BLK):
        xr = x[r].astype(BF16)
        zs = jnp.concatenate([_dot(xr, wc_ref[...]), _dot(xr, ws_ref[...])], axis=0).astype(BF16)
        o_ref[0, r] = _dot(g_ref[r], zs).astype(o_ref.dtype)


def _fm2_kernel(ar_ref, ai_ref, m3_ref, o_ref):
    ar = _swap01(ar_ref[0].astype(F32))
    ai = _swap01(ai_ref[0].astype(F32))
    outs = []
    for q in range(SUB_BLK // 2):
        rhs = jnp.concatenate([ar[2 * q], ai[2 * q], ar[2 * q + 1], ai[2 * q + 1]], axis=0).astype(BF16)
        o2 = _dot(m3_ref[...], rhs)
        outs += [o2[:DFT_INNER], o2[DFT_INNER:]]
    o_ref[0] = _swap01(jnp.stack(outs, axis=0)).astype(o_ref.dtype)


def _fourier_mix(f, tables):
    b, l, fn_w = f.shape
    r = l // DFT_INNER
    wc, ws, g, m3 = tables
    const = lambda bi, j: (0, 0)
    a = pl.pallas_call(
        _fm1_kernel,
        grid=(b, DFT_INNER // SUB_BLK),
        in_specs=[pl.BlockSpec((1, r, SUB_BLK, fn_w), lambda bi, j: (bi, 0, j, 0)),
                  pl.BlockSpec((fn_w, fn_w), const), pl.BlockSpec((fn_w, fn_w), const),
                  pl.BlockSpec((SUB_BLK, 2 * r, 2 * r), lambda bi, j: (j, 0, 0))],
        out_specs=pl.BlockSpec((1, SUB_BLK, 2 * r, fn_w), lambda bi, j: (bi, j, 0, 0)),
        out_shape=jax.ShapeDtypeStruct((b, DFT_INNER, 2 * r, fn_w), BF16),
        compiler_params=_cp("parallel", "parallel"),
        name="fourier_stage1",
    )(f.reshape(b, r, DFT_INNER, fn_w), wc, ws, g)
    nkb = r // SUB_BLK
    out = pl.pallas_call(
        _fm2_kernel,
        grid=(b, nkb),
        in_specs=[pl.BlockSpec((1, DFT_INNER, SUB_BLK, fn_w), lambda bi, j: (bi, 0, j, 0)),
                  pl.BlockSpec((1, DFT_INNER, SUB_BLK, fn_w), lambda bi, j: (bi, 0, nkb + j, 0)),
                  pl.BlockSpec((2 * DFT_INNER, 4 * DFT_INNER), const)],
        out_specs=pl.BlockSpec((1, DFT_INNER, SUB_BLK, fn_w), lambda bi, j: (bi, 0, j, 0)),
        out_shape=jax.ShapeDtypeStruct((b, DFT_INNER, r, fn_w), BF16),
        compiler_params=_cp("parallel", "parallel"),
        name="fourier_stage2",
    )(a, a, m3)
    return out.reshape(b, l, fn_w)


def _halo_ln(xp_ref, x_ref, xn_ref, sh_ref, sc_ref, first, last):
    mod = lambda v: _ln(v) * (1.0 + sc_ref[0]) + sh_ref[0]
    hp = mod(xp_ref[0]) * jnp.where(first, 0.0, 1.0)
    hn = mod(xn_ref[0]) * jnp.where(last, 0.0, 1.0)
    return jnp.concatenate([hp, mod(x_ref[0]), hn], axis=0).astype(BF16)


def _dwconv3(pre, cw, cb):
    n_rows = pre.shape[0]
    up = pltpu.roll(pre, 1, 0)
    dn = pltpu.roll(pre, n_rows - 1, 0)
    u = up * cw[0:1] + pre * cw[1:2] + dn * cw[2:3] + cb
    return u[HALO:n_rows - HALO]


def _halo_specs(tm, d, l, rows=HALO):
    nb = tm // rows
    last_blk = l // rows - 1
    return [pl.BlockSpec((1, rows, d), lambda bi, i, *_: (bi, jnp.maximum(i * nb - 1, 0), 0)),
            pl.BlockSpec((1, tm, d), lambda bi, i, *_: (bi, i, 0)),
            pl.BlockSpec((1, rows, d), lambda bi, i, *_: (bi, jnp.minimum((i + 1) * nb, last_blk), 0))]


def _erf(x):
    return lax.erf(x)


def _mix_ffn_kernel(*refs, n_act, alpha, f, tf):
    acts, ws = refs[:3 * n_act], refs[3 * n_act:4 * n_act]
    (xp_ref, x_ref, xn_ref, gate1_ref, g1_ref, b1_ref, sh_ref, sc_ref, gate2_ref,
     wi_ref, cw_ref, cb_ref, wo_ref, g2_ref, b2_ref, o_ref, z_ref) = refs[4 * n_act:]
    i = pl.program_id(1)

    def mixed(k):
        y = _dot(acts[k][0].astype(BF16), ws[0][...])
        for a in range(1, n_act):
            y = y + _dot(acts[3 * a + k][0].astype(BF16), ws[a][...])
        return y

    res_ln = lambda xv, y: _ln(alpha * xv + gate1_ref[0] * y) * g1_ref[...] + b1_ref[...]
    mod = lambda v: _ln(v) * (1.0 + sc_ref[0]) + sh_ref[0]
    x1 = res_ln(x_ref[0], mixed(1))
    hp = mod(res_ln(xp_ref[0], mixed(0)[ACT_HALO - HALO:])) * jnp.where(i == 0, 0.0, 1.0)
    hn = mod(res_ln(xn_ref[0], mixed(2)[:HALO])) * jnp.where(i == pl.num_programs(1) - 1, 0.0, 1.0)
    h = jnp.concatenate([hp, mod(x1), hn], axis=0).astype(BF16)
    for j in range(f // tf):
        a, g = (_dwconv3(_dot(h, wi_ref[:, lo:lo + tf]), cw_ref[:, lo:lo + tf], cb_ref[:, lo:lo + tf])
                for lo in (j * tf, f + j * tf))
        z_ref[:, j * tf:(j + 1) * tf] = (0.5 * a * (1.0 + _erf(a * (2.0 ** -0.5))) * g).astype(BF16)
    y = _dot(z_ref[...], wo_ref[...])
    zz = alpha * x1 + gate2_ref[0] * y
    o_ref[0] = _ln(zz) * g2_ref[...] + b2_ref[...]


def _mix_ffn(acts, w_mix, x, gate1, ln1_g, ln1_b, sh, sc, gate2, w_in, conv_w, conv_b, w_out,
             ln2_g, ln2_b, alpha, name):
    b, l, d = x.shape
    f = w_out.shape[0]
    tm = _pick(l, (512, 256, 128))
    tf = _pick(f, (256, 128))
    vec = lambda bi, i: (bi, 0, 0)
    resident = lambda shape: pl.BlockSpec(shape, lambda bi, i: (0, 0), pipeline_mode=pl.Buffered(1))
    act_specs, act_args = [], []
    for a in acts:
        act_specs += _halo_specs(tm, a.shape[2], l, ACT_HALO)
        act_args += [a, a, a]
    return pl.pallas_call(
        functools.partial(_mix_ffn_kernel, n_act=len(acts), alpha=alpha, f=f, tf=tf),
        grid=(b, l // tm),
        in_specs=act_specs + [resident(w.shape) for w in w_mix] + _halo_specs(tm, d, l, HALO) + [
            pl.BlockSpec((1, 1, d), vec), resident((1, d)), resident((1, d)),
            pl.BlockSpec((1, 1, d), vec), pl.BlockSpec((1, 1, d), vec), pl.BlockSpec((1, 1, d), vec),
            resident((d, 2 * f)), resident((3, 2 * f)), resident((1, 2 * f)), resident((f, d)),
            resident((1, d)), resident((1, d))],
        out_specs=pl.BlockSpec((1, tm, d), lambda bi, i: (bi, i, 0)),
        out_shape=jax.ShapeDtypeStruct((b, l, d), F32),
        scratch_shapes=[pltpu.VMEM((tm, f), BF16)],
        compiler_params=_cp("parallel", "parallel"),
        name=name,
    )(*act_args, *w_mix, x, x, x, gate1, ln1_g.reshape(1, d), ln1_b.reshape(1, d), sh, sc, gate2,
      w_in, conv_w, conv_b.reshape(1, 2 * f), w_out, ln2_g.reshape(1, d), ln2_b.reshape(1, d))


def _hy_inproj_kernel(xp_ref, x_ref, xn_ref, sh_ref, sc_ref, w_ref, cw_ref, cb_ref,
                      x0_ref, vx_ref, *, d, tn):
    i = pl.program_id(1)
    h = _halo_ln(xp_ref, x_ref, xn_ref, sh_ref, sc_ref, i == 0, i == pl.num_programs(1) - 1)
    for c in range(d // tn):
        def part(k):
            lo = k * d + c * tn
            return _dwconv3(_dot(h, w_ref[:, lo:lo + tn]), cw_ref[:, lo:lo + tn], cb_ref[:, lo:lo + tn])
        x0_ref[0, :, c * tn:(c + 1) * tn] = part(0).astype(BF16)
        vx_ref[0, :, c * tn:(c + 1) * tn] = (part(2) * part(1)).astype(BF16)


def _hy_inproj(x, sh, sc, w, conv_w, conv_b):
    b, l, d = x.shape
    tm = _pick(l, (512, 256, 128))
    tn = _pick(d, (256, 128))
    vec = lambda bi, i: (bi, 0, 0)
    const = lambda bi, i: (0, 0)
    return pl.pallas_call(
        functools.partial(_hy_inproj_kernel, d=d, tn=tn),
        grid=(b, l // tm),
        in_specs=_halo_specs(tm, d, l) + [
            pl.BlockSpec((1, 1, d), vec), pl.BlockSpec((1, 1, d), vec),
            pl.BlockSpec((d, 3 * d), const), pl.BlockSpec((3, 3 * d), const),
            pl.BlockSpec((1, 3 * d), const)],
        out_specs=[pl.BlockSpec((1, tm, d), lambda bi, i: (bi, i, 0)),
                   pl.BlockSpec((1, tm, d), lambda bi, i: (bi, i, 0))],
        out_shape=[jax.ShapeDtypeStruct((b, l, d), BF16), jax.ShapeDtypeStruct((b, l, d), BF16)],
        compiler_params=_cp("parallel", "parallel"),
        name="hyena_inproj",
    )(x, x, x, sh, sc, w, conv_w, conv_b.reshape(1, 3 * d))


def _hy_filter_kernel(z_ref, w1_ref, b1_ref, w2_ref, b2_ref, w3_ref, b3_ref, fr_ref, w4_ref,
                      t_ref, dl_ref, hfb_ref, inv_ref, *, d):
    i = pl.program_id(0)
    fr = fr_ref[...]
    hdn = jnp.sin(fr * (_dot3(z_ref[...], w1_ref[...]) + b1_ref[...]))
    hdn = jnp.sin(fr * (_dot3(hdn, w2_ref[...]) + b2_ref[...]))
    hdn = jnp.sin(fr * (_dot3(hdn, w3_ref[...]) + b3_ref[...]))
    h = _dot3(hdn, w4_ref[...])
    decay = jnp.exp(-t_ref[...] * dl_ref[...])
    hf = h[:, :d] * decay
    hb = h[:, d:] * decay
    row = lax.broadcasted_iota(jnp.int32, hb.shape, 0)
    hb = jnp.where((row == 0) & (i == 0), 0.0, hb)
    hfb_ref[0] = hf
    hfb_ref[1] = hb

    @pl.when(i == 0)
    def _():
        inv_ref[...] = jnp.zeros_like(inv_ref)

    inv_ref[...] += (jnp.sum(jnp.abs(hf), axis=0, keepdims=True)
                     + jnp.sum(jnp.abs(hb), axis=0, keepdims=True))

    @pl.when(i == pl.num_programs(0) - 1)
    def _():
        inv_ref[...] = 1.0 / inv_ref[...]


def _hy_filters(l, d, w1, b1, w2, b2, w3, b3, freq, w4):
    hh = w1.shape[1]
    bands = (HY_FILTER_EMB - 1) // 2
    t = np.linspace(0.0, 1.0, l, dtype=np.float32)[:, None]
    w = (2.0 * math.pi * np.arange(l, dtype=np.float32)[:, None] / l).astype(np.float32)
    fb = np.linspace(1e-4, bands - 1, bands, dtype=np.float32)[None, :]
    z = np.concatenate([t, np.cos(fb * w), -np.sin(fb * w)], axis=-1).astype(np.float32)
    zp = np.zeros((l, V7X_LANES), np.float32)
    zp[:, :HY_FILTER_EMB] = z
    w1p = jnp.zeros((V7X_LANES, hh), F32).at[:HY_FILTER_EMB].set(w1.astype(F32))
    min_decay = math.log(HY_DECAY_TARGET) / HY_SLOW_DECAY
    max_decay = math.log(HY_DECAY_TARGET) / HY_FAST_DECAY
    deltas = np.abs(np.linspace(min_decay, max_decay, d, dtype=np.float32))[None, :]
    tl = _pick(l, (512, 256, 128))
    const = lambda i: (0, 0)
    r2 = lambda a: a.astype(F32).reshape(1, -1)
    return pl.pallas_call(
        functools.partial(_hy_filter_kernel, d=d),
        grid=(l // tl,),
        in_specs=[pl.BlockSpec((tl, V7X_LANES), lambda i: (i, 0)),
                  pl.BlockSpec((V7X_LANES, hh), const), pl.BlockSpec((1, hh), const),
                  pl.BlockSpec((hh, hh), const), pl.BlockSpec((1, hh), const),
                  pl.BlockSpec((hh, hh), const), pl.BlockSpec((1, hh), const),
                  pl.BlockSpec((1, hh), const), pl.BlockSpec((hh, 2 * d), const),
                  pl.BlockSpec((tl, 1), lambda i: (i, 0)), pl.BlockSpec((1, d), const)],
        out_specs=[pl.BlockSpec((2, tl, d), lambda i: (0, i, 0)), pl.BlockSpec((1, d), const)],
        out_shape=[jax.ShapeDtypeStruct((2, l, d), F32), jax.ShapeDtypeStruct((1, d), F32)],
        compiler_params=_cp("arbitrary"),
        name="hyena_filter",
    )(jnp.asarray(zp), w1p, r2(b1), w2.astype(F32), r2(b2), w3.astype(F32), r2(b3), r2(freq),
      w4.astype(F32), jnp.asarray(t), jnp.asarray(deltas))


def _hyena_tables(l):
    r = l // DFT_INNER
    n = 2 * l
    k1 = np.arange(2 * r)[None, :, None]
    n1 = np.arange(r)[None, None, :]
    n2 = np.arange(DFT_INNER)[:, None, None]
    th = 2.0 * np.pi * ((k1 * (DFT_INNER * n1 + n2)) % n) / n
    gr, gi = np.cos(th), -np.sin(th)
    g1r = np.concatenate([gr, gi], 1)
    g1c = np.concatenate([np.concatenate([gr, -gi], 2), np.concatenate([gi, gr], 2)], 1)
    k2 = np.arange(DFT_INNER)
    th3 = 2.0 * np.pi * np.outer(k2, k2) / DFT_INNER
    c, s = np.cos(th3), np.sin(th3)
    m3f = np.kron(np.eye(2), np.block([[c, s], [-s, c]]))
    m3i = np.kron(np.eye(2), np.block([[c, -s], [s, c]]))
    thi = np.transpose(th, (0, 2, 1))
    hr, hi = np.cos(thi), np.sin(thi)
    hc = np.concatenate([np.concatenate([hr, -hi], 2), np.concatenate([hi, hr], 2)], 1) / n
    as_bf = lambda a: jnp.asarray(a, dtype=F32).astype(BF16)
    return as_bf(g1r), as_bf(g1c), as_bf(m3f), as_bf(m3i), as_bf(hc)


def _hy1_kernel(v_ref, g_ref, ar_ref, ai_ref, *, r2):
    xa = _swap01(v_ref[0, 0].astype(F32))
    xb = _swap01(v_ref[0, 1].astype(F32))
    for s in range(SUB_BLK):
        z = jnp.concatenate([xa[s], xb[s]], axis=0).astype(BF16)
        a = _dot(g_ref[s], z)
        ar_ref[0, s] = a[:r2].astype(BF16)
        ai_ref[0, s] = a[r2:].astype(BF16)


def _hy1f_kernel(v_ref, g_ref, ar_ref, ai_ref, *, r2):
    x = _swap01(v_ref[0])
    for s in range(SUB_BLK_F32):
        a = _dot(g_ref[s], x[s].astype(BF16))
        ar_ref[0, s] = a[:r2].astype(BF16)
        ai_ref[0, s] = a[r2:].astype(BF16)


def _hy_stage1(v5, g, td, real_input):
    r, d = v5.shape[-3], v5.shape[-1]
    nseq = v5.shape[0]
    if real_input:
        sub = SUB_BLK_F32
        vspec = pl.BlockSpec((1, r, sub, td), lambda p, j, h: (p, 0, j, h))
        body = _hy1f_kernel
    else:
        sub = SUB_BLK
        vspec = pl.BlockSpec((1, 2, r, sub, td), lambda p, j, h: (p, 0, 0, j, h))
        body = _hy1_kernel
    ospec = pl.BlockSpec((1, sub, 2 * r, td), lambda p, j, h: (p, j, 0, h))
    oshape = jax.ShapeDtypeStruct((nseq, DFT_INNER, 2 * r, d), BF16)
    return pl.pallas_call(
        functools.partial(body, r2=2 * r),
        grid=(nseq, DFT_INNER // sub, d // td),
        in_specs=[vspec, pl.BlockSpec((sub,) + g.shape[1:], lambda p, j, h: (j, 0, 0))],
        out_specs=[ospec, ospec],
        out_shape=[oshape, oshape],
        compiler_params=_cp("parallel", "parallel", "parallel"),
        name="hyena_fft_stage1_filter" if real_input else "hyena_fft_stage1",
    )(v5, g)


def _k1_pair_rhs(fr, fi, q):
    return jnp.concatenate([fr[2 * q], fi[2 * q], fr[2 * q + 1], fi[2 * q + 1]], axis=0).astype(BF16)


def _hy2f_kernel(fr_ref, fi_ref, inv_ref, m3f_ref, kr_ref, ki_ref):
    h = DFT_INNER
    frh, fih = _swap01(fr_ref[0].astype(F32)), _swap01(fi_ref[0].astype(F32))
    frb, fib = _swap01(fr_ref[1].astype(F32)), _swap01(fi_ref[1].astype(F32))
    inv = inv_ref[...]
    for q in range(SUB_BLK // 2):
        sh = _dot(m3f_ref[...], _k1_pair_rhs(frh, fih, q))
        sb = _dot(m3f_ref[...], _k1_pair_rhs(frb, fib, q))
        for u in range(2):
            o = 2 * h * u
            kr_ref[2 * q + u] = ((sh[o:o + h] + sb[o:o + h]) * inv).astype(BF16)
            ki_ref[2 * q + u] = ((sh[o + h:o + 2 * h] - sb[o + h:o + 2 * h]) * inv).astype(BF16)


def _hy_filter_spectrum(afr, afi, inv_norm, m3f, td):
    _, _, r2, d = afr.shape
    fspec = pl.BlockSpec((2, DFT_INNER, SUB_BLK, td), lambda j, h: (0, 0, j, h))
    kspec = pl.BlockSpec((SUB_BLK, DFT_INNER, td), lambda j, h: (j, 0, h))
    kshape = jax.ShapeDtypeStruct((r2, DFT_INNER, d), BF16)
    return pl.pallas_call(
        _hy2f_kernel,
        grid=(r2 // SUB_BLK, d // td),
        in_specs=[fspec, fspec, pl.BlockSpec((1, td), lambda j, h: (0, h)),
                  pl.BlockSpec(m3f.shape, lambda j, h: (0, 0))],
        out_specs=[kspec, kspec],
        out_shape=[kshape, kshape],
        compiler_params=_cp("parallel", "parallel"),
        name="hyena_filter_spectrum",
    )(afr, afi, inv_norm, m3f)


def _hy2_kernel(ar_ref, ai_ref, kr_ref, ki_ref, m3f_ref, m3i_ref, br_ref, bi_ref):
    h = DFT_INNER
    ar, ai = _swap01(ar_ref[0].astype(F32)), _swap01(ai_ref[0].astype(F32))
    brs, bis = [], []
    for q in range(SUB_BLK // 2):
        v = _dot(m3f_ref[...], _k1_pair_rhs(ar, ai, q))
        ys = []
        for u in range(2):
            o = 2 * h * u
            vr, vi = v[o:o + h], v[o + h:o + 2 * h]
            kr, ki = kr_ref[2 * q + u].astype(F32), ki_ref[2 * q + u].astype(F32)
            ys += [vr * kr - vi * ki, vr * ki + vi * kr]
        bn = _dot(m3i_ref[...], jnp.concatenate(ys, axis=0).astype(BF16))
        brs += [bn[0:h], bn[2 * h:3 * h]]
        bis += [bn[h:2 * h], bn[3 * h:4 * h]]
    br_ref[0] = _swap01(jnp.stack(brs, axis=0)).astype(BF16)
    bi_ref[0] = _swap01(jnp.stack(bis, axis=0)).astype(BF16)


def _hy_stage2(ar, ai, kr, ki, m3f, m3i, td):
    p, _, r2, d = ar.shape
    aspec = pl.BlockSpec((1, DFT_INNER, SUB_BLK, td), lambda j, h, pi: (pi, 0, j, h))
    kspec = pl.BlockSpec((SUB_BLK, DFT_INNER, td), lambda j, h, pi: (j, 0, h))
    mspec = pl.BlockSpec(m3f.shape, lambda j, h, pi: (0, 0))
    oshape = jax.ShapeDtypeStruct(ar.shape, BF16)
    return pl.pallas_call(
        _hy2_kernel,
        grid=(r2 // SUB_BLK, d // td, p),
        in_specs=[aspec, aspec, kspec, kspec, mspec, mspec],
        out_specs=[aspec, aspec],
        out_shape=[oshape, oshape],
        compiler_params=_cp("parallel", "parallel", "parallel"),
        name="hyena_fft_stage2",
    )(ar, ai, kr, ki, m3f, m3i)


def _hy3_kernel(br_ref, bi_ref, hc_ref, vx_ref, x0_ref, d_ref, o_ref, *, r):
    ya, yb = [], []
    for s in range(SUB_BLK):
        bb = jnp.concatenate([br_ref[0, s], bi_ref[0, s]], axis=0)
        y = _dot(hc_ref[s], bb)
        ya.append(y[:r])
        yb.append(y[r:])
    for u, ys in enumerate((ya, yb)):
        y = _swap01(jnp.stack(ys, axis=0))
        vx, x0 = vx_ref[0, u].astype(F32), x0_ref[0, u].astype(F32)
        o_ref[0, u] = ((y + vx * d_ref[...]) * x0).astype(o_ref.dtype)


def _hy_stage3(br, bi, hc, vx5, x05, d_skip, td):
    p, _, r, _, d = vx5.shape
    bspec = pl.BlockSpec((1, SUB_BLK, 2 * r, td), lambda pi, j, h: (pi, j, 0, h))
    vspec = pl.BlockSpec((1, 2, r, SUB_BLK, td), lambda pi, j, h: (pi, 0, 0, j, h))
    return pl.pallas_call(
        functools.partial(_hy3_kernel, r=r),
        grid=(p, DFT_INNER // SUB_BLK, d // td),
        in_specs=[bspec, bspec,
                  pl.BlockSpec((SUB_BLK, 2 * r, 4 * r), lambda pi, j, h: (j, 0, 0)),
                  vspec, vspec, pl.BlockSpec((1, 1, td), lambda pi, j, h: (0, 0, h))],
        out_specs=vspec,
        out_shape=jax.ShapeDtypeStruct(vx5.shape, BF16),
        compiler_params=_cp("parallel", "parallel", "parallel"),
        name="hyena_fft_stage3",
    )(br, bi, hc, vx5, x05, d_skip.astype(F32).reshape(1, 1, d))


def _hyena_long_conv(vx, x0, hfb, inv_norm, d_skip, tables):
    b, l, d = vx.shape
    r = l // DFT_INNER
    g1r, g1c, m3f, m3i, hc = tables
    td = _pick(d, (512, 256, 128))
    td_io = _pick(d, (256, 128))
    pair = lambda a: a.reshape(b // 2, 2, r, DFT_INNER, d)
    ar, ai = _hy_stage1(pair(vx), g1c, td_io, real_input=False)
    afr, afi = _hy_stage1(hfb.reshape(2, r, DFT_INNER, d), g1r, td, real_input=True)
    kr, ki = _hy_filter_spectrum(afr, afi, inv_norm, m3f, td)
    br, bi = _hy_stage2(ar, ai, kr, ki, m3f, m3i, td)
    return _hy_stage3(br, bi, hc, pair(vx), pair(x0), d_skip, td_io).reshape(b, l, d)


def _rope_tables(l):
    rows = l // GRID_W
    axis_dim = DA_HEAD_DIM // 2
    pos_r = jnp.repeat(jnp.arange(rows, dtype=F32), GRID_W)
    pos_c = jnp.tile(jnp.arange(GRID_W, dtype=F32), rows)
    inv = ROPE_THETA ** (-jnp.arange(0, axis_dim, 2, dtype=F32) / axis_dim)
    ar, ac = pos_r[:, None] * inv, pos_c[:, None] * inv
    n = ar.shape[1]
    zeros = jnp.zeros((l, n), F32)
    cos64 = jnp.concatenate([jnp.cos(ar), jnp.cos(ar), jnp.cos(ac), jnp.cos(ac)], -1)
    sa64 = jnp.concatenate([-jnp.sin(ar), zeros, -jnp.sin(ac), zeros], -1)
    sb64 = jnp.concatenate([zeros, jnp.sin(ar), zeros, jnp.sin(ac)], -1)
    rep = V7X_LANES // DA_HEAD_DIM
    return jnp.tile(cos64, (1, rep)), jnp.tile(sa64, (1, rep)), jnp.tile(sb64, (1, rep))


def _lambda_init(layer_idx):
    return 0.8 - 0.6 * math.exp(-0.3 * layer_idx)


def kernel(x, c, ctx, c_ctx, mod_w, mod_b, ln1_g, ln1_b, ln2_g, ln2_b, ffn_w_in, ffn_conv_w, ffn_conv_b, ffn_w_out, da_w_in, da_w_out, da_lam_q1, da_lam_k1, da_lam_q2, da_lam_k2, da_subln_g, hy_w_in, hy_conv_w, hy_conv_b, hy_f_w1, hy_f_b1, hy_f_w2, hy_f_b2, hy_f_w3, hy_f_b3, hy_f_freq, hy_f_w4, hy_d, hy_w_out):
    b, l, d = x.shape
    depth = mod_w.shape[0]
    assert depth == 2 and l % ATT_TK == 0 and d % (2 * V7X_LANES) == 0
    assert b % 2 == 0 and (l // DFT_INNER) % SUB_BLK == 0
    da_w = d // 2
    alpha = (2 * depth) ** 0.25

    rows = -(-(b + 1) // 8) * 8
    cv = jnp.zeros((rows, d), F32).at[:b].set(c.astype(F32)).at[b].set(c_ctx.astype(F32))
    mod = _mod_params(cv, mod_w.astype(F32), mod_b.astype(F32))

    def mods(i):
        m = mod[i].reshape(rows, N_MOD, d)
        return [m[:b, k][:, None, :] for k in range(N_MOD)], [m[b:b + 1, k][:, None, :] for k in range(N_MOD)]

    (sh1, sc1, g1, sh2, sc2, g2), (csh1, csc1, _, _, _, _) = mods(0)
    w_in = da_w_in[0].astype(BF16)
    cos, sa, sb = _rope_tables(l)
    qT, k, vT, f = _inproj(x, sh1, sc1, w_in, cos, sa, sb, da_w)
    kc, vcT = _ctx_kv(ctx, csh1, csc1, w_in[:, da_w:3 * da_w], da_w)
    lam_rows = jnp.zeros((8, V7X_LANES), F32)
    for r_i, p in enumerate((da_lam_q1[0], da_lam_k1[0], da_lam_q2[0], da_lam_k2[0])):
        lam_rows = lam_rows.at[r_i, :DA_HEAD_DIM].set(p.astype(F32))
    o = _diff_attention(qT, k, vT, kc, vcT, lam_rows, da_subln_g[0].astype(F32).reshape(1, -1), _lambda_init(0))
    fm = _fourier_mix(f, _fourier_tables(l, d - da_w))
    w_out = da_w_out[0].astype(BF16)
    x = _mix_ffn([o, fm], [w_out[:da_w], w_out[da_w:]], x, g1, ln1_g[0], ln1_b[0], sh2, sc2, g2,
                 ffn_w_in[0].astype(BF16), ffn_conv_w[0].astype(F32), ffn_conv_b[0].astype(F32),
                 ffn_w_out[0].astype(BF16), ln2_g[0], ln2_b[0], alpha, "l0_mix_ffn")

    (sh1, sc1, g1, sh2, sc2, g2), _ = mods(1)
    x0, vx = _hy_inproj(x, sh1, sc1, hy_w_in[0].astype(BF16), hy_conv_w[0].astype(F32),
                        hy_conv_b[0].astype(F32))
    hfb, inv_norm = _hy_filters(l, d, hy_f_w1[0], hy_f_b1[0], hy_f_w2[0], hy_f_b2[0],
                                hy_f_w3[0], hy_f_b3[0], hy_f_freq[0], hy_f_w4[0])
    z = _hyena_long_conv(vx, x0, hfb, inv_norm, hy_d[0], _hyena_tables(l))
    return _mix_ffn([z], [hy_w_out[0].astype(BF16)], x, g1, ln1_g[1], ln1_b[1], sh2, sc2, g2,
                    ffn_w_in[1].astype(BF16), ffn_conv_w[1].astype(F32), ffn_conv_b[1].astype(F32),
                    ffn_w_out[1].astype(BF16), ln2_g[1], ln2_b[1], alpha, "l1_mix_ffn")
```

```python
import functools
import math

import numpy as np
import jax
import jax.numpy as jnp
from jax import lax
from jax.experimental import pallas as pl
from jax.experimental.pallas import tpu as pltpu

F32 = jnp.float32
BF16 = jnp.bfloat16

GRID_W = 64
DA_HEAD_DIM = 64
ROPE_THETA = 10000.0
N_MOD = 6
LN_EPS = 1e-5
FN_GROUPS = 4
HY_FILTER_EMB = 33
HY_DECAY_TARGET = 1e-2
HY_FAST_DECAY = 0.3
HY_SLOW_DECAY = 1.5
DFT_INNER = GRID_W

V7X_LANES = 128
V7X_MXU_DEPTH = 256
V7X_VMEM_LIMIT_BYTES = 56 * 1024 * 1024
HALO = 8
ACT_HALO = 16
HEAD_W = 2 * DA_HEAD_DIM
ONES_ROWS = 16
ATT_TK = 512
ATT_UNROLL = 8
SUB_BLK = 16
SUB_BLK_F32 = 8


def _cp(*sem):
    return pltpu.CompilerParams(dimension_semantics=sem, vmem_limit_bytes=V7X_VMEM_LIMIT_BYTES)


def _dot(a, b):
    return jnp.dot(a, b, preferred_element_type=F32)


def _split(a):
    hi = a.astype(BF16)
    lo = (a - hi.astype(F32)).astype(BF16)
    return hi, lo


def _dot3(a, b):
    ah, al = _split(a)
    bh, bl = _split(b)
    return _dot(ah, bh) + _dot(al, bh) + _dot(ah, bl)


def _ln(x):
    mu = jnp.mean(x, axis=-1, keepdims=True)
    xc = x - mu
    var = jnp.mean(xc * xc, axis=-1, keepdims=True)
    return xc * lax.rsqrt(var + LN_EPS)


def _pick(n, prefs):
    for p in prefs:
        if n % p == 0:
            return p
    return n


def _mod_kernel(cv_ref, w_ref, b_ref, o_ref):
    cv = cv_ref[...]
    s = cv * jax.nn.sigmoid(cv)
    o_ref[0] = _dot3(s, w_ref[0]) + b_ref[0]


def _mod_params(cv, mod_w, mod_b):
    depth, d, n = mod_w.shape
    rows = cv.shape[0]
    tn = _pick(n, (1536, 1024, 512, 256, 128))
    return pl.pallas_call(
        _mod_kernel,
        grid=(depth, n // tn),
        in_specs=[pl.BlockSpec((rows, d), lambda i, j: (0, 0)),
                  pl.BlockSpec((1, d, tn), lambda i, j: (i, 0, j)),
                  pl.BlockSpec((1, 1, tn), lambda i, j: (i, 0, j))],
        out_specs=pl.BlockSpec((1, rows, tn), lambda i, j: (i, 0, j)),
        out_shape=jax.ShapeDtypeStruct((depth, rows, n), F32),
        compiler_params=_cp("parallel", "parallel"),
        name="mod_params",
    )(cv, mod_w, mod_b.reshape(depth, 1, n))


def _rope_block(blk, cos, sin_a, sin_b):
    return (blk * cos + pltpu.roll(blk, V7X_LANES - 16, 1) * sin_a + pltpu.roll(blk, 16, 1) * sin_b)


def _store_vT(v, vT_ref, head):
    vT_ref[0, head, 0, 0:HEAD_W, :] = v.T.astype(BF16)
    vT_ref[0, head, 0, HEAD_W:, :] = jnp.ones((ONES_ROWS, v.shape[0]), BF16)


def _inproj_kernel(x_ref, sh_ref, sc_ref, w_ref, cos_ref, sa_ref, sb_ref,
                   qT_ref, k_ref, vT_ref, f_ref, *, da_w, q_scale):
    h = (_ln(x_ref[0]) * (1.0 + sc_ref[0]) + sh_ref[0]).astype(BF16)
    p = _dot(h, w_ref[...])
    cos, sa, sb = cos_ref[...], sa_ref[...], sb_ref[...]
    for j in range(da_w // HEAD_W):
        lo, hi = j * HEAD_W, (j + 1) * HEAD_W
        qT_ref[0, lo:hi, :] = (_rope_block(p[:, lo:hi], cos, sa, sb) * q_scale).T.astype(BF16)
        k_ref[0, :, lo:hi] = _rope_block(p[:, da_w + lo:da_w + hi], cos, sa, sb).astype(BF16)
        _store_vT(p[:, 2 * da_w + lo:2 * da_w + hi], vT_ref, j)
    f_ref[0] = p[:, 3 * da_w:].astype(BF16)


def _inproj(x, sh, sc, w, cos, sa, sb, da_w):
    b, l, d = x.shape
    n = w.shape[1]
    fn_w = n - 3 * da_w
    heads = da_w // HEAD_W
    tm = ATT_TK
    row = lambda bi, i: (bi, i, 0)
    vec = lambda bi, i: (bi, 0, 0)
    tab = lambda bi, i: (i, 0)
    return pl.pallas_call(
        functools.partial(_inproj_kernel, da_w=da_w, q_scale=DA_HEAD_DIM ** -0.5 * math.log2(math.e)),
        grid=(b, l // tm),
        in_specs=[pl.BlockSpec((1, tm, d), row),
                  pl.BlockSpec((1, 1, d), vec), pl.BlockSpec((1, 1, d), vec),
                  pl.BlockSpec((d, n), lambda bi, i: (0, 0)),
                  pl.BlockSpec((tm, V7X_LANES), tab), pl.BlockSpec((tm, V7X_LANES), tab),
                  pl.BlockSpec((tm, V7X_LANES), tab)],
        out_specs=[pl.BlockSpec((1, da_w, tm), lambda bi, i: (bi, 0, i)),
                   pl.BlockSpec((1, tm, da_w), row),
                   pl.BlockSpec((1, heads, 1, HEAD_W + ONES_ROWS, tm), lambda bi, i: (bi, 0, i, 0, 0)),
                   pl.BlockSpec((1, tm, fn_w), row)],
        out_shape=[jax.ShapeDtypeStruct((b, da_w, l), BF16), jax.ShapeDtypeStruct((b, l, da_w), BF16),
                   jax.ShapeDtypeStruct((b, heads, l // tm, HEAD_W + ONES_ROWS, tm), BF16),
                   jax.ShapeDtypeStruct((b, l, fn_w), BF16)],
        compiler_params=_cp("parallel", "parallel"),
        name="l0_inproj",
    )(x, sh, sc, w, cos, sa, sb)


def _ctx_kv_kernel(x_ref, sh_ref, sc_ref, w_ref, k_ref, vT_ref, *, da_w):
    h = (_ln(x_ref[0]) * (1.0 + sc_ref[0]) + sh_ref[0]).astype(BF16)
    p = _dot(h, w_ref[...])
    k_ref[0] = p[:, :da_w].astype(BF16)
    for j in range(da_w // HEAD_W):
        _store_vT(p[:, da_w + j * HEAD_W:da_w + (j + 1) * HEAD_W], vT_ref, j)


def _ctx_kv(ctx, sh, sc, w, da_w):
    b, lc, d = ctx.shape
    heads = da_w // HEAD_W
    return pl.pallas_call(
        functools.partial(_ctx_kv_kernel, da_w=da_w),
        grid=(b,),
        in_specs=[pl.BlockSpec((1, lc, d), lambda bi: (bi, 0, 0)),
                  pl.BlockSpec((1, 1, d), lambda bi: (0, 0, 0)),
                  pl.BlockSpec((1, 1, d), lambda bi: (0, 0, 0)),
                  pl.BlockSpec((d, 2 * da_w), lambda bi: (0, 0))],
        out_specs=[pl.BlockSpec((1, lc, da_w), lambda bi: (bi, 0, 0)),
                   pl.BlockSpec((1, heads, 1, HEAD_W + ONES_ROWS, lc), lambda bi: (bi, 0, 0, 0, 0))],
        out_shape=[jax.ShapeDtypeStruct((b, lc, da_w), BF16),
                   jax.ShapeDtypeStruct((b, heads, 1, HEAD_W + ONES_ROWS, lc), BF16)],
        compiler_params=_cp("parallel"),
        name="ctx_kv_proj",
    )(ctx, sh, sc, w)


def _attn_kernel(qT_ref, kc_ref, vcT_ref, k_ref, vT_ref, lam_ref, g_ref, o_ref,
                 m1_ref, a1_ref, m2_ref, a2_ref, sa1_ref, sa2_ref, sb1_ref, sb2_ref, *, lam_init):
    qT = qT_ref[0]
    row = lax.broadcasted_iota(jnp.int32, qT.shape, 0)
    zero = jnp.zeros_like(qT)
    q1 = jnp.where(row < DA_HEAD_DIM, qT, zero)
    q2 = jnp.where(row >= DA_HEAD_DIM, qT, zero)
    nk = vT_ref.shape[2]

    m1_ref[...] = jnp.full_like(m1_ref, -jnp.inf)
    m2_ref[...] = jnp.full_like(m2_ref, -jnp.inf)
    a1_ref[...] = jnp.zeros_like(a1_ref)
    a2_ref[...] = jnp.zeros_like(a2_ref)

    def softmax_pv(s_ref, rows, vb, m_ref, a_ref):
        m_old = m_ref[...]
        m_new = jnp.maximum(m_old, jnp.max(s_ref[0:rows, :], axis=0, keepdims=True))
        acc = a_ref[...] * jnp.exp2(m_old - m_new)
        step = min(rows, V7X_MXU_DEPTH)
        for lo in range(0, rows, step):
            p = jnp.exp2(s_ref[lo:lo + step, :] - m_new).astype(BF16)
            acc = acc + _dot(vb[:, lo:lo + step], p)
        a_ref[...] = acc
        m_ref[...] = m_new

    def scores(t, s1_ref, s2_ref):
        off = pl.multiple_of(t * ATT_TK, ATT_TK)
        kb = k_ref[0, pl.ds(off, ATT_TK), :]
        s1_ref[...] = _dot(kb, q1)
        s2_ref[...] = _dot(kb, q2)

    def consume(t, s1_ref, s2_ref):
        vb = vT_ref[0, 0, t]
        softmax_pv(s1_ref, ATT_TK, vb, m1_ref, a1_ref)
        softmax_pv(s2_ref, ATT_TK, vb, m2_ref, a2_ref)

    bufs = ((sa1_ref, sa2_ref), (sb1_ref, sb2_ref))

    kc, vc = kc_ref[0], vcT_ref[0, 0, 0]
    lc = kc.shape[0]
    sb1_ref[0:lc, :] = _dot(kc, q1)
    sb2_ref[0:lc, :] = _dot(kc, q2)
    scores(0, sa1_ref, sa2_ref)
    softmax_pv(sb1_ref, lc, vc, m1_ref, a1_ref)
    softmax_pv(sb2_ref, lc, vc, m2_ref, a2_ref)


    unroll = math.gcd(nk, ATT_UNROLL)
    assert unroll % 2 == 0

    def body(j, carry):
        for u in range(unroll):
            t = unroll * j + u
            scores(jnp.minimum(t + 1, nk - 1), *bufs[(u + 1) % 2])
            consume(t, *bufs[u % 2])
        return carry

    lax.fori_loop(0, nk // unroll, body, 0)

    lp = lam_ref[...]
    lam = (jnp.exp(jnp.sum(lp[0:1] * lp[1:2], axis=-1, keepdims=True))
           - jnp.exp(jnp.sum(lp[2:3] * lp[3:4], axis=-1, keepdims=True)) + lam_init)
    a1, a2 = a1_ref[...], a2_ref[...]
    o = (a1[:HEAD_W] * (1.0 / a1[HEAD_W:HEAD_W + 1])
         - lam * (a2[:HEAD_W] * (1.0 / a2[HEAD_W:HEAD_W + 1])))
    ms = jnp.mean(o * o, axis=0, keepdims=True)
    on = (o * lax.rsqrt(ms + LN_EPS)).T
    o_ref[0] = (on * g_ref[...] * (1.0 - lam_init)).astype(o_ref.dtype)


def _diff_attention(qT, k, vT, kc, vcT, lam_rows, subln_g, lam_init):
    b, da_w, l = qT.shape
    lc = kc.shape[1]
    heads = da_w // HEAD_W
    nk = vT.shape[2]
    av = HEAD_W + ONES_ROWS
    tq = _pick(l, (512, 256, 128))
    kmap = lambda bi, h, i: (bi, 0, h)
    vmap = lambda bi, h, i: (bi, h, 0, 0, 0)
    return pl.pallas_call(
        functools.partial(_attn_kernel, lam_init=lam_init),
        grid=(b, heads, l // tq),
        in_specs=[pl.BlockSpec((1, HEAD_W, tq), lambda bi, h, i: (bi, h, i)),
                  pl.BlockSpec((1, lc, HEAD_W), kmap),
                  pl.BlockSpec((1, 1, 1, av, lc), vmap),
                  pl.BlockSpec((1, l, HEAD_W), kmap),
                  pl.BlockSpec((1, 1, nk, av, ATT_TK), vmap),
                  pl.BlockSpec((8, V7X_LANES), lambda bi, h, i: (0, 0)),
                  pl.BlockSpec((1, HEAD_W), lambda bi, h, i: (0, 0))],
        out_specs=pl.BlockSpec((1, tq, HEAD_W), lambda bi, h, i: (bi, i, h)),
        out_shape=jax.ShapeDtypeStruct((b, l, da_w), BF16),
        scratch_shapes=[pltpu.VMEM((1, tq), F32), pltpu.VMEM((av, tq), F32),
                        pltpu.VMEM((1, tq), F32), pltpu.VMEM((av, tq), F32)]
                       + [pltpu.VMEM((ATT_TK, tq), F32)] * 4,
        compiler_params=_cp("parallel", "parallel", "parallel"),
        name="diff_attention",
    )(qT, kc, vcT, k, vT, lam_rows, subln_g)


def _fourier_tables(l, fn_w):
    r = l // DFT_INNER
    gw = fn_w // FN_GROUPS
    cc = np.arange(gw)
    th = 2.0 * np.pi * np.outer(cc, cc) / gw
    eye = np.eye(FN_GROUPS)
    wc = np.kron(eye, np.cos(th) / np.sqrt(gw))
    ws = np.kron(eye, -np.sin(th) / np.sqrt(gw))
    k1 = np.arange(r)[None, :, None]
    n1 = np.arange(r)[None, None, :]
    n2 = np.arange(DFT_INNER)[:, None, None]
    th1 = 2.0 * np.pi * ((k1 * (DFT_INNER * n1 + n2)) % l) / l
    gr, gi = np.cos(th1), -np.sin(th1)
    g = np.concatenate([np.concatenate([gr, -gi], 2), np.concatenate([gi, gr], 2)], 1)
    k2 = np.arange(DFT_INNER)
    th3 = 2.0 * np.pi * np.outer(k2, k2) / DFT_INNER
    m3 = np.concatenate([np.cos(th3), np.sin(th3)], 1) / np.sqrt(l)
    m3 = np.kron(np.eye(2), m3)
    as_bf = lambda a: jnp.asarray(a, dtype=F32).astype(BF16)
    return as_bf(wc), as_bf(ws), as_bf(g), as_bf(m3)


def _swap01(x):
    return jnp.swapaxes(x, 0, 1)


def _regroup_perm():
    p = np.zeros((HALO * SUB_BLK, HALO * SUB_BLK), np.float32)
    for k in range(SUB_BLK):
        for r in range(HALO):
            p[k * HALO + r, r * SUB_BLK + k] = 1.0
    return jnp.asarray(p).astype(BF16), jnp.asarray(p.T).astype(BF16)


def _to_lead(x, perm):
    n = x.shape[2]
    parts = [_dot(perm, x[HALO * a:HALO * (a + 1)].reshape(HALO * SUB_BLK, n))
             for a in range(x.shape[0] // HALO)]
    return [jnp.concatenate([p[HALO * k:HALO * (k + 1)] for p in parts], axis=0) for k in range(SUB_BLK)]


def _from_lead(arrs, perm_t):
    n = arrs[0].shape[1]
    outs = []
    for a in range(arrs[0].shape[0] // HALO):
        ya = jnp.concatenate([v[HALO * a:HALO * (a + 1)] for v in arrs], axis=0).astype(BF16)
        outs.append(_dot(perm_t, ya).reshape(HALO, SUB_BLK, n))
    return jnp.concatenate(outs, axis=0).astype(BF16)


def _fm1_kernel(x_ref, wc_ref, ws_ref, g_ref, o_ref):
    x = _swap01(x_ref[0].astype(F32))
    for r in range(SUB_BLK):
        xr = x[r].astype(BF16)
        zs = jnp.concatenate([_dot(xr, wc_ref[...]), _dot(xr, ws_ref[...])], axis=0).astype(BF16)
        o_ref[0, r] = _dot(g_ref[r], zs).astype(o_ref.dtype)


def _fm2_kernel(ar_ref, ai_ref, m3_ref, pm_ref, pmt_ref, o_ref):
    ar = _to_lead(ar_ref[0], pm_ref[...])
    ai = _to_lead(ai_ref[0], pm_ref[...])
    outs = []
    for q in range(SUB_BLK // 2):
        o2 = _dot(m3_ref[...], _k1_pair_rhs(ar, ai, q))
        outs += [o2[:DFT_INNER], o2[DFT_INNER:]]
    o_ref[0] = _from_lead(outs, pmt_ref[...])


def _fourier_mix(f, tables):
    b, l, fn_w = f.shape
    r = l // DFT_INNER
    wc, ws, g, m3 = tables
    pm, pmt = _regroup_perm()
    const = lambda bi, j: (0, 0)
    a = pl.pallas_call(
        _fm1_kernel,
        grid=(b, DFT_INNER // SUB_BLK),
        in_specs=[pl.BlockSpec((1, r, SUB_BLK, fn_w), lambda bi, j: (bi, 0, j, 0)),
                  pl.BlockSpec((fn_w, fn_w), const), pl.BlockSpec((fn_w, fn_w), const),
                  pl.BlockSpec((SUB_BLK, 2 * r, 2 * r), lambda bi, j: (j, 0, 0))],
        out_specs=pl.BlockSpec((1, SUB_BLK, 2 * r, fn_w), lambda bi, j: (bi, j, 0, 0)),
        out_shape=jax.ShapeDtypeStruct((b, DFT_INNER, 2 * r, fn_w), BF16),
        compiler_params=_cp("parallel", "parallel"),
        name="fourier_stage1",
    )(f.reshape(b, r, DFT_INNER, fn_w), wc, ws, g)
    nkb = r // SUB_BLK
    out = pl.pallas_call(
        _fm2_kernel,
        grid=(b, nkb),
        in_specs=[pl.BlockSpec((1, DFT_INNER, SUB_BLK, fn_w), lambda bi, j: (bi, 0, j, 0)),
                  pl.BlockSpec((1, DFT_INNER, SUB_BLK, fn_w), lambda bi, j: (bi, 0, nkb + j, 0)),
                  pl.BlockSpec((2 * DFT_INNER, 4 * DFT_INNER), const),
                  pl.BlockSpec(pm.shape, const), pl.BlockSpec(pm.shape, const)],
        out_specs=pl.BlockSpec((1, DFT_INNER, SUB_BLK, fn_w), lambda bi, j: (bi, 0, j, 0)),
        out_shape=jax.ShapeDtypeStruct((b, DFT_INNER, r, fn_w), BF16),
        compiler_params=_cp("parallel", "parallel"),
        name="fourier_stage2",
    )(a, a, m3, pm, pmt)
    return out.reshape(b, l, fn_w)


def _halo_ln(xp_ref, x_ref, xn_ref, sh_ref, sc_ref, first, last):
    mod = lambda v: _ln(v) * (1.0 + sc_ref[0]) + sh_ref[0]
    hp = mod(xp_ref[0]) * jnp.where(first, 0.0, 1.0)
    hn = mod(xn_ref[0]) * jnp.where(last, 0.0, 1.0)
    return jnp.concatenate([hp, mod(x_ref[0]), hn], axis=0).astype(BF16)


def _dwconv3(pre, cw, cb):
    n_rows = pre.shape[0]
    up = pltpu.roll(pre, 1, 0)
    dn = pltpu.roll(pre, n_rows - 1, 0)
    u = up * cw[0:1] + pre * cw[1:2] + dn * cw[2:3] + cb
    return u[HALO:n_rows - HALO]


def _halo_specs(tm, d, l, rows=HALO):
    nb = tm // rows
    last_blk = l // rows - 1
    return [pl.BlockSpec((1, rows, d), lambda bi, i, *_: (bi, jnp.maximum(i * nb - 1, 0), 0)),
            pl.BlockSpec((1, tm, d), lambda bi, i, *_: (bi, i, 0)),
            pl.BlockSpec((1, rows, d), lambda bi, i, *_: (bi, jnp.minimum((i + 1) * nb, last_blk), 0))]


def _erf(x):
    return lax.erf(x)


def _mix_ffn_kernel(*refs, n_act, alpha, f, tf):
    acts, ws = refs[:3 * n_act], refs[3 * n_act:4 * n_act]
    (xp_ref, x_ref, xn_ref, gate1_ref, g1_ref, b1_ref, sh_ref, sc_ref, gate2_ref,
     wi_ref, cw_ref, cb_ref, wo_ref, g2_ref, b2_ref, o_ref, z_ref) = refs[4 * n_act:]
    i = pl.program_id(1)

    def mixed(k):
        y = _dot(acts[k][0].astype(BF16), ws[0][...])
        for a in range(1, n_act):
            y = y + _dot(acts[3 * a + k][0].astype(BF16), ws[a][...])
        return y

    res_ln = lambda xv, y: _ln(alpha * xv + gate1_ref[0] * y) * g1_ref[...] + b1_ref[...]
    mod = lambda v: _ln(v) * (1.0 + sc_ref[0]) + sh_ref[0]
    x1 = res_ln(x_ref[0], mixed(1))
    hp = mod(res_ln(xp_ref[0], mixed(0)[ACT_HALO - HALO:])) * jnp.where(i == 0, 0.0, 1.0)
    hn = mod(res_ln(xn_ref[0], mixed(2)[:HALO])) * jnp.where(i == pl.num_programs(1) - 1, 0.0, 1.0)
    h = jnp.concatenate([hp, mod(x1), hn], axis=0).astype(BF16)
    for j in range(f // tf):
        a, g = (_dwconv3(_dot(h, wi_ref[:, lo:lo + tf]), cw_ref[:, lo:lo + tf], cb_ref[:, lo:lo + tf])
                for lo in (j * tf, f + j * tf))
        z_ref[:, j * tf:(j + 1) * tf] = (0.5 * a * (1.0 + _erf(a * (2.0 ** -0.5))) * g).astype(BF16)
    y = _dot(z_ref[...], wo_ref[...])
    zz = alpha * x1 + gate2_ref[0] * y
    o_ref[0] = _ln(zz) * g2_ref[...] + b2_ref[...]


def _mix_ffn(acts, w_mix, x, gate1, ln1_g, ln1_b, sh, sc, gate2, w_in, conv_w, conv_b, w_out,
             ln2_g, ln2_b, alpha, name):
    b, l, d = x.shape
    f = w_out.shape[0]
    tm = _pick(l, (512, 256, 128))
    tf = _pick(f, (256, 128))
    vec = lambda bi, i: (bi, 0, 0)
    resident = lambda shape: pl.BlockSpec(shape, lambda bi, i: (0, 0), pipeline_mode=pl.Buffered(1))
    act_specs, act_args = [], []
    for a in acts:
        act_specs += _halo_specs(tm, a.shape[2], l, ACT_HALO)
        act_args += [a, a, a]
    return pl.pallas_call(
        functools.partial(_mix_ffn_kernel, n_act=len(acts), alpha=alpha, f=f, tf=tf),
        grid=(b, l // tm),
        in_specs=act_specs + [resident(w.shape) for w in w_mix] + _halo_specs(tm, d, l, HALO) + [
            pl.BlockSpec((1, 1, d), vec), resident((1, d)), resident((1, d)),
            pl.BlockSpec((1, 1, d), vec), pl.BlockSpec((1, 1, d), vec), pl.BlockSpec((1, 1, d), vec),
            resident((d, 2 * f)), resident((3, 2 * f)), resident((1, 2 * f)), resident((f, d)),
            resident((1, d)), resident((1, d))],
        out_specs=pl.BlockSpec((1, tm, d), lambda bi, i: (bi, i, 0)),
        out_shape=jax.ShapeDtypeStruct((b, l, d), F32),
        scratch_shapes=[pltpu.VMEM((tm, f), BF16)],
        compiler_params=_cp("parallel", "parallel"),
        name=name,
    )(*act_args, *w_mix, x, x, x, gate1, ln1_g.reshape(1, d), ln1_b.reshape(1, d), sh, sc, gate2,
      w_in, conv_w, conv_b.reshape(1, 2 * f), w_out, ln2_g.reshape(1, d), ln2_b.reshape(1, d))


def _hy_inproj_kernel(xp_ref, x_ref, xn_ref, sh_ref, sc_ref, w_ref, cw_ref, cb_ref,
                      x0_ref, vx_ref, *, d, tn):
    i = pl.program_id(1)
    h = _halo_ln(xp_ref, x_ref, xn_ref, sh_ref, sc_ref, i == 0, i == pl.num_programs(1) - 1)
    for c in range(d // tn):
        def part(k):
            lo = k * d + c * tn
            return _dwconv3(_dot(h, w_ref[:, lo:lo + tn]), cw_ref[:, lo:lo + tn], cb_ref[:, lo:lo + tn])
        x0_ref[0, :, c * tn:(c + 1) * tn] = part(0).astype(BF16)
        vx_ref[0, :, c * tn:(c + 1) * tn] = (part(2) * part(1)).astype(BF16)


def _hy_inproj(x, sh, sc, w, conv_w, conv_b):
    b, l, d = x.shape
    tm = _pick(l, (512, 256, 128))
    tn = _pick(d, (256, 128))
    vec = lambda bi, i: (bi, 0, 0)
    const = lambda bi, i: (0, 0)
    return pl.pallas_call(
        functools.partial(_hy_inproj_kernel, d=d, tn=tn),
        grid=(b, l // tm),
        in_specs=_halo_specs(tm, d, l) + [
            pl.BlockSpec((1, 1, d), vec), pl.BlockSpec((1, 1, d), vec),
            pl.BlockSpec((d, 3 * d), const), pl.BlockSpec((3, 3 * d), const),
            pl.BlockSpec((1, 3 * d), const)],
        out_specs=[pl.BlockSpec((1, tm, d), lambda bi, i: (bi, i, 0)),
                   pl.BlockSpec((1, tm, d), lambda bi, i: (bi, i, 0))],
        out_shape=[jax.ShapeDtypeStruct((b, l, d), BF16), jax.ShapeDtypeStruct((b, l, d), BF16)],
        compiler_params=_cp("parallel", "parallel"),
        name="hyena_inproj",
    )(x, x, x, sh, sc, w, conv_w, conv_b.reshape(1, 3 * d))


def _hy_filter_kernel(z_ref, w1_ref, b1_ref, w2_ref, b2_ref, w3_ref, b3_ref, fr_ref, w4_ref,
                      t_ref, dl_ref, hfb_ref, inv_ref, *, d):
    i = pl.program_id(0)
    fr = fr_ref[...]
    hdn = jnp.sin(fr * (_dot3(z_ref[...], w1_ref[...]) + b1_ref[...]))
    hdn = jnp.sin(fr * (_dot3(hdn, w2_ref[...]) + b2_ref[...]))
    hdn = jnp.sin(fr * (_dot3(hdn, w3_ref[...]) + b3_ref[...]))
    h = _dot3(hdn, w4_ref[...])
    decay = jnp.exp(-t_ref[...] * dl_ref[...])
    hf = h[:, :d] * decay
    hb = h[:, d:] * decay
    row = lax.broadcasted_iota(jnp.int32, hb.shape, 0)
    hb = jnp.where((row == 0) & (i == 0), 0.0, hb)
    hfb_ref[0] = hf
    hfb_ref[1] = hb

    @pl.when(i == 0)
    def _():
        inv_ref[...] = jnp.zeros_like(inv_ref)

    inv_ref[...] += (jnp.sum(jnp.abs(hf), axis=0, keepdims=True)
                     + jnp.sum(jnp.abs(hb), axis=0, keepdims=True))

    @pl.when(i == pl.num_programs(0) - 1)
    def _():
        inv_ref[...] = 1.0 / inv_ref[...]


def _hy_filters(l, d, w1, b1, w2, b2, w3, b3, freq, w4):
    hh = w1.shape[1]
    bands = (HY_FILTER_EMB - 1) // 2
    t = np.linspace(0.0, 1.0, l, dtype=np.float32)[:, None]
    w = (2.0 * math.pi * np.arange(l, dtype=np.float32)[:, None] / l).astype(np.float32)
    fb = np.linspace(1e-4, bands - 1, bands, dtype=np.float32)[None, :]
    z = np.concatenate([t, np.cos(fb * w), -np.sin(fb * w)], axis=-1).astype(np.float32)
    zp = np.zeros((l, V7X_LANES), np.float32)
    zp[:, :HY_FILTER_EMB] = z
    w1p = jnp.zeros((V7X_LANES, hh), F32).at[:HY_FILTER_EMB].set(w1.astype(F32))
    min_decay = math.log(HY_DECAY_TARGET) / HY_SLOW_DECAY
    max_decay = math.log(HY_DECAY_TARGET) / HY_FAST_DECAY
    deltas = np.abs(np.linspace(min_decay, max_decay, d, dtype=np.float32))[None, :]
    tl = _pick(l, (512, 256, 128))
    const = lambda i: (0, 0)
    r2 = lambda a: a.astype(F32).reshape(1, -1)
    return pl.pallas_call(
        functools.partial(_hy_filter_kernel, d=d),
        grid=(l // tl,),
        in_specs=[pl.BlockSpec((tl, V7X_LANES), lambda i: (i, 0)),
                  pl.BlockSpec((V7X_LANES, hh), const), pl.BlockSpec((1, hh), const),
                  pl.BlockSpec((hh, hh), const), pl.BlockSpec((1, hh), const),
                  pl.BlockSpec((hh, hh), const), pl.BlockSpec((1, hh), const),
                  pl.BlockSpec((1, hh), const), pl.BlockSpec((hh, 2 * d), const),
                  pl.BlockSpec((tl, 1), lambda i: (i, 0)), pl.BlockSpec((1, d), const)],
        out_specs=[pl.BlockSpec((2, tl, d), lambda i: (0, i, 0)), pl.BlockSpec((1, d), const)],
        out_shape=[jax.ShapeDtypeStruct((2, l, d), F32), jax.ShapeDtypeStruct((1, d), F32)],
        compiler_params=_cp("arbitrary"),
        name="hyena_filter",
    )(jnp.asarray(zp), w1p, r2(b1), w2.astype(F32), r2(b2), w3.astype(F32), r2(b3), r2(freq),
      w4.astype(F32), jnp.asarray(t), jnp.asarray(deltas))


def _hyena_tables(l):
    r = l // DFT_INNER
    n = 2 * l
    k1 = np.arange(2 * r)[None, :, None]
    n1 = np.arange(r)[None, None, :]
    n2 = np.arange(DFT_INNER)[:, None, None]
    th = 2.0 * np.pi * ((k1 * (DFT_INNER * n1 + n2)) % n) / n
    gr, gi = np.cos(th), -np.sin(th)
    g1r = np.concatenate([gr, gi], 1)
    g1c = np.concatenate([np.concatenate([gr, -gi], 2), np.concatenate([gi, gr], 2)], 1)
    k2 = np.arange(DFT_INNER)
    th3 = 2.0 * np.pi * np.outer(k2, k2) / DFT_INNER
    c, s = np.cos(th3), np.sin(th3)
    m3f = np.kron(np.eye(2), np.block([[c, s], [-s, c]]))
    m3i = np.kron(np.eye(2), np.block([[c, -s], [s, c]]))
    thi = np.transpose(th, (0, 2, 1))
    hr, hi = np.cos(thi), np.sin(thi)
    hc = np.concatenate([np.concatenate([hr, -hi], 2), np.concatenate([hi, hr], 2)], 1) / n
    as_bf = lambda a: jnp.asarray(a, dtype=F32).astype(BF16)
    return as_bf(g1r), as_bf(g1c), as_bf(m3f), as_bf(m3i), as_bf(hc)


def _hy1_kernel(v_ref, g_ref, ar_ref, ai_ref, *, r2):
    xa = _swap01(v_ref[0, 0].astype(F32))
    xb = _swap01(v_ref[0, 1].astype(F32))
    for s in range(SUB_BLK):
        z = jnp.concatenate([xa[s], xb[s]], axis=0).astype(BF16)
        a = _dot(g_ref[s], z)
        ar_ref[0, s] = a[:r2].astype(BF16)
        ai_ref[0, s] = a[r2:].astype(BF16)


def _hy1f_kernel(v_ref, g_ref, ar_ref, ai_ref, *, r2):
    x = _swap01(v_ref[0])
    for s in range(SUB_BLK_F32):
        a = _dot(g_ref[s], x[s].astype(BF16))
        ar_ref[0, s] = a[:r2].astype(BF16)
        ai_ref[0, s] = a[r2:].astype(BF16)


def _hy_stage1(v5, g, td, real_input):
    r, d = v5.shape[-3], v5.shape[-1]
    nseq = v5.shape[0]
    if real_input:
        sub = SUB_BLK_F32
        vspec = pl.BlockSpec((1, r, sub, td), lambda p, j, h: (p, 0, j, h))
        body = _hy1f_kernel
    else:
        sub = SUB_BLK
        vspec = pl.BlockSpec((1, 2, r, sub, td), lambda p, j, h: (p, 0, 0, j, h))
        body = _hy1_kernel
    ospec = pl.BlockSpec((1, sub, 2 * r, td), lambda p, j, h: (p, j, 0, h))
    oshape = jax.ShapeDtypeStruct((nseq, DFT_INNER, 2 * r, d), BF16)
    return pl.pallas_call(
        functools.partial(body, r2=2 * r),
        grid=(nseq, DFT_INNER // sub, d // td),
        in_specs=[vspec, pl.BlockSpec((sub,) + g.shape[1:], lambda p, j, h: (j, 0, 0))],
        out_specs=[ospec, ospec],
        out_shape=[oshape, oshape],
        compiler_params=_cp("parallel", "parallel", "parallel"),
        name="hyena_fft_stage1_filter" if real_input else "hyena_fft_stage1",
    )(v5, g)


def _k1_pair_rhs(fr, fi, q):
    return jnp.concatenate([fr[2 * q], fi[2 * q], fr[2 * q + 1], fi[2 * q + 1]], axis=0).astype(BF16)


def _hy2f_kernel(fr_ref, fi_ref, inv_ref, m3f_ref, pm_ref, kr_ref, ki_ref):
    h = DFT_INNER
    frh, fih = _to_lead(fr_ref[0], pm_ref[...]), _to_lead(fi_ref[0], pm_ref[...])
    frb, fib = _to_lead(fr_ref[1], pm_ref[...]), _to_lead(fi_ref[1], pm_ref[...])
    inv = inv_ref[...]
    for q in range(SUB_BLK // 2):
        sh = _dot(m3f_ref[...], _k1_pair_rhs(frh, fih, q))
        sb = _dot(m3f_ref[...], _k1_pair_rhs(frb, fib, q))
        for u in range(2):
            o = 2 * h * u
            kr_ref[2 * q + u] = ((sh[o:o + h] + sb[o:o + h]) * inv).astype(BF16)
            ki_ref[2 * q + u] = ((sh[o + h:o + 2 * h] - sb[o + h:o + 2 * h]) * inv).astype(BF16)


def _hy_filter_spectrum(afr, afi, inv_norm, m3f, pm, td):
    _, _, r2, d = afr.shape
    fspec = pl.BlockSpec((2, DFT_INNER, SUB_BLK, td), lambda j, h: (0, 0, j, h))
    kspec = pl.BlockSpec((SUB_BLK, DFT_INNER, td), lambda j, h: (j, 0, h))
    kshape = jax.ShapeDtypeStruct((r2, DFT_INNER, d), BF16)
    return pl.pallas_call(
        _hy2f_kernel,
        grid=(r2 // SUB_BLK, d // td),
        in_specs=[fspec, fspec, pl.BlockSpec((1, td), lambda j, h: (0, h)),
                  pl.BlockSpec(m3f.shape, lambda j, h: (0, 0)), pl.BlockSpec(pm.shape, lambda j, h: (0, 0))],
        out_specs=[kspec, kspec],
        out_shape=[kshape, kshape],
        compiler_params=_cp("parallel", "parallel"),
        name="hyena_filter_spectrum",
    )(afr, afi, inv_norm, m3f, pm)


def _hy2_kernel(ar_ref, ai_ref, kr_ref, ki_ref, m3f_ref, m3i_ref, pm_ref, br_ref, bi_ref):
    h = DFT_INNER
    ar, ai = _to_lead(ar_ref[0], pm_ref[...]), _to_lead(ai_ref[0], pm_ref[...])
    nq, td = SUB_BLK // 2, ar_ref.shape[3]
    v = _dot(m3f_ref[...], jnp.concatenate([_k1_pair_rhs(ar, ai, q) for q in range(nq)], axis=1))
    cols = []
    for q in range(nq):
        ys = []
        for u in range(2):
            o = 2 * h * u
            vr, vi = v[o:o + h, q * td:(q + 1) * td], v[o + h:o + 2 * h, q * td:(q + 1) * td]
            kr, ki = kr_ref[2 * q + u].astype(F32), ki_ref[2 * q + u].astype(F32)
            ys += [vr * kr - vi * ki, vr * ki + vi * kr]
        cols.append(jnp.concatenate(ys, axis=0).astype(BF16))
    bn = _dot(m3i_ref[...], jnp.concatenate(cols, axis=1))
    brs, bis = [], []
    for q in range(nq):
        bq = bn[:, q * td:(q + 1) * td]
        brs += [bq[0:h], bq[2 * h:3 * h]]
        bis += [bq[h:2 * h], bq[3 * h:4 * h]]
    br_ref[0] = _swap01(jnp.stack(brs, axis=0)).astype(BF16)
    bi_ref[0] = _swap01(jnp.stack(bis, axis=0)).astype(BF16)


def _hy_stage2(ar, ai, kr, ki, m3f, m3i, pm, td):
    p, _, r2, d = ar.shape
    aspec = pl.BlockSpec((1, DFT_INNER, SUB_BLK, td), lambda j, h, pi: (pi, 0, j, h))
    kspec = pl.BlockSpec((SUB_BLK, DFT_INNER, td), lambda j, h, pi: (j, 0, h))
    mspec = pl.BlockSpec(m3f.shape, lambda j, h, pi: (0, 0))
    oshape = jax.ShapeDtypeStruct(ar.shape, BF16)
    return pl.pallas_call(
        _hy2_kernel,
        grid=(r2 // SUB_BLK, d // td, p),
        in_specs=[aspec, aspec, kspec, kspec, mspec, mspec, pl.BlockSpec(pm.shape, lambda j, h, pi: (0, 0))],
        out_specs=[aspec, aspec],
        out_shape=[oshape, oshape],
        compiler_params=_cp("parallel", "parallel", "parallel"),
        name="hyena_fft_stage2",
    )(ar, ai, kr, ki, m3f, m3i, pm)


def _hy3_kernel(br_ref, bi_ref, hc_ref, vx_ref, x0_ref, d_ref, o_ref, *, r):
    ya, yb = [], []
    for s in range(SUB_BLK):
        bb = jnp.concatenate([br_ref[0, s], bi_ref[0, s]], axis=0)
        y = _dot(hc_ref[s], bb)
        ya.append(y[:r])
        yb.append(y[r:])
    for u, ys in enumerate((ya, yb)):
        y = _swap01(jnp.stack(ys, axis=0))
        vx, x0 = vx_ref[0, u].astype(F32), x0_ref[0, u].astype(F32)
        o_ref[0, u] = ((y + vx * d_ref[...]) * x0).astype(o_ref.dtype)


def _hy_stage3(br, bi, hc, vx5, x05, d_skip, td):
    p, _, r, _, d = vx5.shape
    bspec = pl.BlockSpec((1, SUB_BLK, 2 * r, td), lambda pi, j, h: (pi, j, 0, h))
    vspec = pl.BlockSpec((1, 2, r, SUB_BLK, td), lambda pi, j, h: (pi, 0, 0, j, h))
    return pl.pallas_call(
        functools.partial(_hy3_kernel, r=r),
        grid=(p, DFT_INNER // SUB_BLK, d // td),
        in_specs=[bspec, bspec,
                  pl.BlockSpec((SUB_BLK, 2 * r, 4 * r), lambda pi, j, h: (j, 0, 0)),
                  vspec, vspec, pl.BlockSpec((1, 1, td), lambda pi, j, h: (0, 0, h))],
        out_specs=vspec,
        out_shape=jax.ShapeDtypeStruct(vx5.shape, BF16),
        compiler_params=_cp("parallel", "parallel", "parallel"),
        name="hyena_fft_stage3",
    )(br, bi, hc, vx5, x05, d_skip.astype(F32).reshape(1, 1, d))


def _hyena_long_conv(vx, x0, hfb, inv_norm, d_skip, tables):
    b, l, d = vx.shape
    r = l // DFT_INNER
    g1r, g1c, m3f, m3i, hc = tables
    td = _pick(d, (512, 256, 128))
    td_io = _pick(d, (256, 128))
    pair = lambda a: a.reshape(b // 2, 2, r, DFT_INNER, d)
    ar, ai = _hy_stage1(pair(vx), g1c, td_io, real_input=False)
    afr, afi = _hy_stage1(hfb.reshape(2, r, DFT_INNER, d), g1r, td, real_input=True)
    pm, _ = _regroup_perm()
    kr, ki = _hy_filter_spectrum(afr, afi, inv_norm, m3f, pm, td)
    br, bi = _hy_stage2(ar, ai, kr, ki, m3f, m3i, pm, td)
    return _hy_stage3(br, bi, hc, pair(vx), pair(x0), d_skip, td_io).reshape(b, l, d)


def _rope_tables(l):
    rows = l // GRID_W
    axis_dim = DA_HEAD_DIM // 2
    pos_r = jnp.repeat(jnp.arange(rows, dtype=F32), GRID_W)
    pos_c = jnp.tile(jnp.arange(GRID_W, dtype=F32), rows)
    inv = ROPE_THETA ** (-jnp.arange(0, axis_dim, 2, dtype=F32) / axis_dim)
    ar, ac = pos_r[:, None] * inv, pos_c[:, None] * inv
    n = ar.shape[1]
    zeros = jnp.zeros((l, n), F32)
    cos64 = jnp.concatenate([jnp.cos(ar), jnp.cos(ar), jnp.cos(ac), jnp.cos(ac)], -1)
    sa64 = jnp.concatenate([-jnp.sin(ar), zeros, -jnp.sin(ac), zeros], -1)
    sb64 = jnp.concatenate([zeros, jnp.sin(ar), zeros, jnp.sin(ac)], -1)
    rep = V7X_LANES // DA_HEAD_DIM
    return jnp.tile(cos64, (1, rep)), jnp.tile(sa64, (1, rep)), jnp.tile(sb64, (1, rep))


def _lambda_init(layer_idx):
    return 0.8 - 0.6 * math.exp(-0.3 * layer_idx)


def kernel(x, c, ctx, c_ctx, mod_w, mod_b, ln1_g, ln1_b, ln2_g, ln2_b, ffn_w_in, ffn_conv_w, ffn_conv_b, ffn_w_out, da_w_in, da_w_out, da_lam_q1, da_lam_k1, da_lam_q2, da_lam_k2, da_subln_g, hy_w_in, hy_conv_w, hy_conv_b, hy_f_w1, hy_f_b1, hy_f_w2, hy_f_b2, hy_f_w3, hy_f_b3, hy_f_freq, hy_f_w4, hy_d, hy_w_out):
    b, l, d = x.shape
    depth = mod_w.shape[0]
    assert depth == 2 and l % ATT_TK == 0 and d % (2 * V7X_LANES) == 0
    assert b % 2 == 0 and (l // DFT_INNER) % SUB_BLK == 0
    da_w = d // 2
    alpha = (2 * depth) ** 0.25

    rows = -(-(b + 1) // 8) * 8
    cv = jnp.zeros((rows, d), F32).at[:b].set(c.astype(F32)).at[b].set(c_ctx.astype(F32))
    mod = _mod_params(cv, mod_w.astype(F32), mod_b.astype(F32))

    def mods(i):
        m = mod[i].reshape(rows, N_MOD, d)
        return [m[:b, k][:, None, :] for k in range(N_MOD)], [m[b:b + 1, k][:, None, :] for k in range(N_MOD)]

    (sh1, sc1, g1, sh2, sc2, g2), (csh1, csc1, _, _, _, _) = mods(0)
    w_in = da_w_in[0].astype(BF16)
    cos, sa, sb = _rope_tables(l)
    qT, k, vT, f = _inproj(x, sh1, sc1, w_in, cos, sa, sb, da_w)
    kc, vcT = _ctx_kv(ctx, csh1, csc1, w_in[:, da_w:3 * da_w], da_w)
    lam_rows = jnp.zeros((8, V7X_LANES), F32)
    for r_i, p in enumerate((da_lam_q1[0], da_lam_k1[0], da_lam_q2[0], da_lam_k2[0])):
        lam_rows = lam_rows.at[r_i, :DA_HEAD_DIM].set(p.astype(F32))
    o = _diff_attention(qT, k, vT, kc, vcT, lam_rows, da_subln_g[0].astype(F32).reshape(1, -1), _lambda_init(0))
    fm = _fourier_mix(f, _fourier_tables(l, d - da_w))
    w_out = da_w_out[0].astype(BF16)
    x = _mix_ffn([o, fm], [w_out[:da_w], w_out[da_w:]], x, g1, ln1_g[0], ln1_b[0], sh2, sc2, g2,
                 ffn_w_in[0].astype(BF16), ffn_conv_w[0].astype(F32), ffn_conv_b[0].astype(F32),
                 ffn_w_out[0].astype(BF16), ln2_g[0], ln2_b[0], alpha, "l0_mix_ffn")

    (sh1, sc1, g1, sh2, sc2, g2), _ = mods(1)
    x0, vx = _hy_inproj(x, sh1, sc1, hy_w_in[0].astype(BF16), hy_conv_w[0].astype(F32),
                        hy_conv_b[0].astype(F32))
    hfb, inv_norm = _hy_filters(l, d, hy_f_w1[0], hy_f_b1[0], hy_f_w2[0], hy_f_b2[0],
                                hy_f_w3[0], hy_f_b3[0], hy_f_freq[0], hy_f_w4[0])
    z = _hyena_long_conv(vx, x0, hfb, inv_norm, hy_d[0], _hyena_tables(l))
    return _mix_ffn([z], [hy_w_out[0].astype(BF16)], x, g1, ln1_g[1], ln1_b[1], sh2, sc2, g2,
                    ffn_w_in[1].astype(BF16), ffn_conv_w[1].astype(F32), ffn_conv_b[1].astype(F32),
                    ffn_w_out[1].astype(BF16), ln2_g[1], ln2_b[1], alpha, "l1_mix_ffn")
```

```python
import functools
import math

import numpy as np
import jax
import jax.numpy as jnp
from jax import lax
from jax.experimental import pallas as pl
from jax.experimental.pallas import tpu as pltpu

F32 = jnp.float32
BF16 = jnp.bfloat16

GRID_W = 64
DA_HEAD_DIM = 64
ROPE_THETA = 10000.0
N_MOD = 6
LN_EPS = 1e-5
FN_GROUPS = 4
HY_FILTER_EMB = 33
HY_DECAY_TARGET = 1e-2
HY_FAST_DECAY = 0.3
HY_SLOW_DECAY = 1.5
DFT_INNER = GRID_W

V7X_LANES = 128
V7X_MXU_DEPTH = 256
V7X_VMEM_LIMIT_BYTES = 56 * 1024 * 1024
HALO = 8
ACT_HALO = 16
HEAD_W = 2 * DA_HEAD_DIM
ONES_ROWS = 16
ATT_TK = 512
ATT_UNROLL = 8
SUB_BLK = 16
SUB_BLK_F32 = 8


def _cp(*sem):
    return pltpu.CompilerParams(dimension_semantics=sem, vmem_limit_bytes=V7X_VMEM_LIMIT_BYTES)


def _dot(a, b):
    return jnp.dot(a, b, preferred_element_type=F32)


def _split(a):
    hi = a.astype(BF16)
    lo = (a - hi.astype(F32)).astype(BF16)
    return hi, lo


def _dot3(a, b):
    ah, al = _split(a)
    bh, bl = _split(b)
    return _dot(ah, bh) + _dot(al, bh) + _dot(ah, bl)


def _ln(x):
    mu = jnp.mean(x, axis=-1, keepdims=True)
    xc = x - mu
    var = jnp.mean(xc * xc, axis=-1, keepdims=True)
    return xc * lax.rsqrt(var + LN_EPS)


def _pick(n, prefs):
    for p in prefs:
        if n % p == 0:
            return p
    return n


def _mod_kernel(cv_ref, w_ref, b_ref, o_ref):
    cv = cv_ref[...]
    s = cv * jax.nn.sigmoid(cv)
    o_ref[0] = _dot3(s, w_ref[0]) + b_ref[0]


def _mod_params(cv, mod_w, mod_b):
    depth, d, n = mod_w.shape
    rows = cv.shape[0]
    tn = _pick(n, (1536, 1024, 512, 256, 128))
    return pl.pallas_call(
        _mod_kernel,
        grid=(depth, n // tn),
        in_specs=[pl.BlockSpec((rows, d), lambda i, j: (0, 0)),
                  pl.BlockSpec((1, d, tn), lambda i, j: (i, 0, j)),
                  pl.BlockSpec((1, 1, tn), lambda i, j: (i, 0, j))],
        out_specs=pl.BlockSpec((1, rows, tn), lambda i, j: (i, 0, j)),
        out_shape=jax.ShapeDtypeStruct((depth, rows, n), F32),
        compiler_params=_cp("parallel", "parallel"),
        name="mod_params",
    )(cv, mod_w, mod_b.reshape(depth, 1, n))


def _rope_block(blk, cos, sin_a, sin_b):
    return (blk * cos + pltpu.roll(blk, V7X_LANES - 16, 1) * sin_a + pltpu.roll(blk, 16, 1) * sin_b)


def _store_vT(v, vT_ref, head):
    vT_ref[0, head, 0, 0:HEAD_W, :] = v.T.astype(BF16)
    vT_ref[0, head, 0, HEAD_W:, :] = jnp.ones((ONES_ROWS, v.shape[0]), BF16)


def _inproj_kernel(x_ref, sh_ref, sc_ref, w_ref, cos_ref, sa_ref, sb_ref,
                   qT_ref, k_ref, vT_ref, f_ref, *, da_w, q_scale):
    h = (_ln(x_ref[0]) * (1.0 + sc_ref[0]) + sh_ref[0]).astype(BF16)
    p = _dot(h, w_ref[...])
    cos, sa, sb = cos_ref[...], sa_ref[...], sb_ref[...]
    for j in range(da_w // HEAD_W):
        lo, hi = j * HEAD_W, (j + 1) * HEAD_W
        qT_ref[0, lo:hi, :] = (_rope_block(p[:, lo:hi], cos, sa, sb) * q_scale).T.astype(BF16)
        k_ref[0, :, lo:hi] = _rope_block(p[:, da_w + lo:da_w + hi], cos, sa, sb).astype(BF16)
        _store_vT(p[:, 2 * da_w + lo:2 * da_w + hi], vT_ref, j)
    f_ref[0] = p[:, 3 * da_w:].astype(BF16)


def _inproj(x, sh, sc, w, cos, sa, sb, da_w):
    b, l, d = x.shape
    n = w.shape[1]
    fn_w = n - 3 * da_w
    heads = da_w // HEAD_W
    tm = ATT_TK
    row = lambda bi, i: (bi, i, 0)
    vec = lambda bi, i: (bi, 0, 0)
    tab = lambda bi, i: (i, 0)
    return pl.pallas_call(
        functools.partial(_inproj_kernel, da_w=da_w, q_scale=DA_HEAD_DIM ** -0.5 * math.log2(math.e)),
        grid=(b, l // tm),
        in_specs=[pl.BlockSpec((1, tm, d), row),
                  pl.BlockSpec((1, 1, d), vec), pl.BlockSpec((1, 1, d), vec),
                  pl.BlockSpec((d, n), lambda bi, i: (0, 0)),
                  pl.BlockSpec((tm, V7X_LANES), tab), pl.BlockSpec((tm, V7X_LANES), tab),
                  pl.BlockSpec((tm, V7X_LANES), tab)],
        out_specs=[pl.BlockSpec((1, da_w, tm), lambda bi, i: (bi, 0, i)),
                   pl.BlockSpec((1, tm, da_w), row),
                   pl.BlockSpec((1, heads, 1, HEAD_W + ONES_ROWS, tm), lambda bi, i: (bi, 0, i, 0, 0)),
                   pl.BlockSpec((1, tm, fn_w), row)],
        out_shape=[jax.ShapeDtypeStruct((b, da_w, l), BF16), jax.ShapeDtypeStruct((b, l, da_w), BF16),
                   jax.ShapeDtypeStruct((b, heads, l // tm, HEAD_W + ONES_ROWS, tm), BF16),
                   jax.ShapeDtypeStruct((b, l, fn_w), BF16)],
        compiler_params=_cp("parallel", "parallel"),
        name="l0_inproj",
    )(x, sh, sc, w, cos, sa, sb)


def _ctx_kv_kernel(x_ref, sh_ref, sc_ref, w_ref, k_ref, vT_ref, *, da_w):
    h = (_ln(x_ref[0]) * (1.0 + sc_ref[0]) + sh_ref[0]).astype(BF16)
    p = _dot(h, w_ref[...])
    k_ref[0] = p[:, :da_w].astype(BF16)
    for j in range(da_w // HEAD_W):
        _store_vT(p[:, da_w + j * HEAD_W:da_w + (j + 1) * HEAD_W], vT_ref, j)


def _ctx_kv(ctx, sh, sc, w, da_w):
    b, lc, d = ctx.shape
    heads = da_w // HEAD_W
    return pl.pallas_call(
        functools.partial(_ctx_kv_kernel, da_w=da_w),
        grid=(b,),
        in_specs=[pl.BlockSpec((1, lc, d), lambda bi: (bi, 0, 0)),
                  pl.BlockSpec((1, 1, d), lambda bi: (0, 0, 0)),
                  pl.BlockSpec((1, 1, d), lambda bi: (0, 0, 0)),
                  pl.BlockSpec((d, 2 * da_w), lambda bi: (0, 0))],
        out_specs=[pl.BlockSpec((1, lc, da_w), lambda bi: (bi, 0, 0)),
                   pl.BlockSpec((1, heads, 1, HEAD_W + ONES_ROWS, lc), lambda bi: (bi, 0, 0, 0, 0))],
        out_shape=[jax.ShapeDtypeStruct((b, lc, da_w), BF16),
                   jax.ShapeDtypeStruct((b, heads, 1, HEAD_W + ONES_ROWS, lc), BF16)],
        compiler_params=_cp("parallel"),
        name="ctx_kv_proj",
    )(ctx, sh, sc, w)


def _attn_kernel(qT_ref, kc_ref, vcT_ref, k_ref, vT_ref, lam_ref, g_ref, o_ref,
                 m1_ref, a1_ref, m2_ref, a2_ref, sa1_ref, sa2_ref, sb1_ref, sb2_ref, *, lam_init):
    qT = qT_ref[0]
    row = lax.broadcasted_iota(jnp.int32, qT.shape, 0)
    zero = jnp.zeros_like(qT)
    q1 = jnp.where(row < DA_HEAD_DIM, qT, zero)
    q2 = jnp.where(row >= DA_HEAD_DIM, qT, zero)
    nk = vT_ref.shape[2]

    m1_ref[...] = jnp.full_like(m1_ref, -jnp.inf)
    m2_ref[...] = jnp.full_like(m2_ref, -jnp.inf)
    a1_ref[...] = jnp.zeros_like(a1_ref)
    a2_ref[...] = jnp.zeros_like(a2_ref)

    def softmax_pv(s_ref, rows, vb, m_ref, a_ref):
        m_old = m_ref[...]
        m_new = jnp.maximum(m_old, jnp.max(s_ref[0:rows, :], axis=0, keepdims=True))
        acc = a_ref[...] * jnp.exp2(m_old - m_new)
        step = min(rows, V7X_MXU_DEPTH)
        for lo in range(0, rows, step):
            p = jnp.exp2(s_ref[lo:lo + step, :] - m_new).astype(BF16)
            acc = acc + _dot(vb[:, lo:lo + step], p)
        a_ref[...] = acc
        m_ref[...] = m_new

    def scores(t, s1_ref, s2_ref):
        off = pl.multiple_of(t * ATT_TK, ATT_TK)
        kb = k_ref[0, pl.ds(off, ATT_TK), :]
        s1_ref[...] = _dot(kb, q1)
        s2_ref[...] = _dot(kb, q2)

    def consume(t, s1_ref, s2_ref):
        vb = vT_ref[0, 0, t]
        softmax_pv(s1_ref, ATT_TK, vb, m1_ref, a1_ref)
        softmax_pv(s2_ref, ATT_TK, vb, m2_ref, a2_ref)

    bufs = ((sa1_ref, sa2_ref), (sb1_ref, sb2_ref))

    kc, vc = kc_ref[0], vcT_ref[0, 0, 0]
    lc = kc.shape[0]
    sb1_ref[0:lc, :] = _dot(kc, q1)
    sb2_ref[0:lc, :] = _dot(kc, q2)
    scores(0, sa1_ref, sa2_ref)
    softmax_pv(sb1_ref, lc, vc, m1_ref, a1_ref)
    softmax_pv(sb2_ref, lc, vc, m2_ref, a2_ref)


    unroll = math.gcd(nk, ATT_UNROLL)
    assert unroll % 2 == 0

    def body(j, carry):
        for u in range(unroll):
            t = unroll * j + u
            scores(jnp.minimum(t + 1, nk - 1), *bufs[(u + 1) % 2])
            consume(t, *bufs[u % 2])
        return carry

    lax.fori_loop(0, nk // unroll, body, 0)

    lp = lam_ref[...]
    lam = (jnp.exp(jnp.sum(lp[0:1] * lp[1:2], axis=-1, keepdims=True))
           - jnp.exp(jnp.sum(lp[2:3] * lp[3:4], axis=-1, keepdims=True)) + lam_init)
    a1, a2 = a1_ref[...], a2_ref[...]
    o = (a1[:HEAD_W] * (1.0 / a1[HEAD_W:HEAD_W + 1])
         - lam * (a2[:HEAD_W] * (1.0 / a2[HEAD_W:HEAD_W + 1])))
    ms = jnp.mean(o * o, axis=0, keepdims=True)
    on = (o * lax.rsqrt(ms + LN_EPS)).T
    o_ref[0] = (on * g_ref[...] * (1.0 - lam_init)).astype(o_ref.dtype)


def _diff_attention(qT, k, vT, kc, vcT, lam_rows, subln_g, lam_init):
    b, da_w, l = qT.shape
    lc = kc.shape[1]
    heads = da_w // HEAD_W
    nk = vT.shape[2]
    av = HEAD_W + ONES_ROWS
    tq = _pick(l, (512, 256, 128))
    kmap = lambda bi, h, i: (bi, 0, h)
    vmap = lambda bi, h, i: (bi, h, 0, 0, 0)
    return pl.pallas_call(
        functools.partial(_attn_kernel, lam_init=lam_init),
        grid=(b, heads, l // tq),
        in_specs=[pl.BlockSpec((1, HEAD_W, tq), lambda bi, h, i: (bi, h, i)),
                  pl.BlockSpec((1, lc, HEAD_W), kmap),
                  pl.BlockSpec((1, 1, 1, av, lc), vmap),
                  pl.BlockSpec((1, l, HEAD_W), kmap),
                  pl.BlockSpec((1, 1, nk, av, ATT_TK), vmap),
                  pl.BlockSpec((8, V7X_LANES), lambda bi, h, i: (0, 0)),
                  pl.BlockSpec((1, HEAD_W), lambda bi, h, i: (0, 0))],
        out_specs=pl.BlockSpec((1, tq, HEAD_W), lambda bi, h, i: (bi, i, h)),
        out_shape=jax.ShapeDtypeStruct((b, l, da_w), BF16),
        scratch_shapes=[pltpu.VMEM((1, tq), F32), pltpu.VMEM((av, tq), F32),
                        pltpu.VMEM((1, tq), F32), pltpu.VMEM((av, tq), F32)]
                       + [pltpu.VMEM((ATT_TK, tq), F32)] * 4,
        compiler_params=_cp("parallel", "parallel", "parallel"),
        name="diff_attention",
    )(qT, kc, vcT, k, vT, lam_rows, subln_g)


def _fourier_tables(l, fn_w):
    r = l // DFT_INNER
    gw = fn_w // FN_GROUPS
    cc = np.arange(gw)
    th = 2.0 * np.pi * np.outer(cc, cc) / gw
    wcs = np.concatenate([np.cos(th), -np.sin(th)], 1) / np.sqrt(gw)
    k1 = np.arange(r)[None, :, None]
    n1 = np.arange(r)[None, None, :]
    n2 = np.arange(DFT_INNER)[:, None, None]
    th1 = 2.0 * np.pi * ((k1 * (DFT_INNER * n1 + n2)) % l) / l
    gr, gi = np.cos(th1), -np.sin(th1)
    g = np.concatenate([np.concatenate([gr, -gi], 2), np.concatenate([gi, gr], 2)], 1)
    k2 = np.arange(DFT_INNER)
    th3 = 2.0 * np.pi * np.outer(k2, k2) / DFT_INNER
    m3 = np.concatenate([np.cos(th3), np.sin(th3)], 1) / np.sqrt(l)
    m3 = np.kron(np.eye(2), m3)
    as_bf = lambda a: jnp.asarray(a, dtype=F32).astype(BF16)
    return as_bf(wcs), as_bf(g), as_bf(m3)


def _swap01(x):
    return jnp.swapaxes(x, 0, 1)


def _regroup_perm():
    p = np.zeros((HALO * SUB_BLK, HALO * SUB_BLK), np.float32)
    for k in range(SUB_BLK):
        for r in range(HALO):
            p[k * HALO + r, r * SUB_BLK + k] = 1.0
    return jnp.asarray(p).astype(BF16), jnp.asarray(p.T).astype(BF16)


def _to_lead(x, perm):
    n = x.shape[2]
    parts = [_dot(perm, x[HALO * a:HALO * (a + 1)].reshape(HALO * SUB_BLK, n))
             for a in range(x.shape[0] // HALO)]
    return [jnp.concatenate([p[HALO * k:HALO * (k + 1)] for p in parts], axis=0) for k in range(SUB_BLK)]


def _from_lead(arrs, perm_t):
    n = arrs[0].shape[1]
    outs = []
    for a in range(arrs[0].shape[0] // HALO):
        ya = jnp.concatenate([v[HALO * a:HALO * (a + 1)] for v in arrs], axis=0).astype(BF16)
        outs.append(_dot(perm_t, ya).reshape(HALO, SUB_BLK, n))
    return jnp.concatenate(outs, axis=0).astype(BF16)


def _fm1_kernel(x_ref, wcs_ref, g_ref, o_ref):
    x = _swap01(x_ref[0].astype(F32))
    gw = wcs_ref.shape[0]
    for r in range(SUB_BLK):
        xr = x[r].astype(BF16)
        zz = [_dot(xr[:, c:c + gw], wcs_ref[...]) for c in range(0, xr.shape[1], gw)]
        zs = jnp.concatenate([jnp.concatenate([z[:, :gw] for z in zz], axis=1),
                              jnp.concatenate([z[:, gw:] for z in zz], axis=1)], axis=0).astype(BF16)
        o_ref[0, r] = _dot(g_ref[r], zs).astype(o_ref.dtype)


def _fm2_kernel(ar_ref, ai_ref, m3_ref, pm_ref, pmt_ref, o_ref):
    ar = _to_lead(ar_ref[0], pm_ref[...])
    ai = _to_lead(ai_ref[0], pm_ref[...])
    outs = []
    for q in range(SUB_BLK // 2):
        o2 = _dot(m3_ref[...], _k1_pair_rhs(ar, ai, q))
        outs += [o2[:DFT_INNER], o2[DFT_INNER:]]
    o_ref[0] = _from_lead(outs, pmt_ref[...])


def _fourier_mix(f, tables):
    b, l, fn_w = f.shape
    r = l // DFT_INNER
    wcs, g, m3 = tables
    pm, pmt = _regroup_perm()
    const = lambda bi, j: (0, 0)
    a = pl.pallas_call(
        _fm1_kernel,
        grid=(b, DFT_INNER // SUB_BLK),
        in_specs=[pl.BlockSpec((1, r, SUB_BLK, fn_w), lambda bi, j: (bi, 0, j, 0)),
                  pl.BlockSpec(wcs.shape, const),
                  pl.BlockSpec((SUB_BLK, 2 * r, 2 * r), lambda bi, j: (j, 0, 0))],
        out_specs=pl.BlockSpec((1, SUB_BLK, 2 * r, fn_w), lambda bi, j: (bi, j, 0, 0)),
        out_shape=jax.ShapeDtypeStruct((b, DFT_INNER, 2 * r, fn_w), BF16),
        compiler_params=_cp("parallel", "parallel"),
        name="fourier_stage1",
    )(f.reshape(b, r, DFT_INNER, fn_w), wcs, g)
    nkb = r // SUB_BLK
    out = pl.pallas_call(
        _fm2_kernel,
        grid=(b, nkb),
        in_specs=[pl.BlockSpec((1, DFT_INNER, SUB_BLK, fn_w), lambda bi, j: (bi, 0, j, 0)),
                  pl.BlockSpec((1, DFT_INNER, SUB_BLK, fn_w), lambda bi, j: (bi, 0, nkb + j, 0)),
                  pl.BlockSpec((2 * DFT_INNER, 4 * DFT_INNER), const),
                  pl.BlockSpec(pm.shape, const), pl.BlockSpec(pm.shape, const)],
        out_specs=pl.BlockSpec((1, DFT_INNER, SUB_BLK, fn_w), lambda bi, j: (bi, 0, j, 0)),
        out_shape=jax.ShapeDtypeStruct((b, DFT_INNER, r, fn_w), BF16),
        compiler_params=_cp("parallel", "parallel"),
        name="fourier_stage2",
    )(a, a, m3, pm, pmt)
    return out.reshape(b, l, fn_w)


def _halo_ln(xp_ref, x_ref, xn_ref, sh_ref, sc_ref, first, last):
    mod = lambda v: _ln(v) * (1.0 + sc_ref[0]) + sh_ref[0]
    hp = mod(xp_ref[0]) * jnp.where(first, 0.0, 1.0)
    hn = mod(xn_ref[0]) * jnp.where(last, 0.0, 1.0)
    return jnp.concatenate([hp, mod(x_ref[0]), hn], axis=0).astype(BF16)


def _dwconv3(pre, cw, cb):
    n_rows = pre.shape[0]
    up = pltpu.roll(pre, 1, 0)
    dn = pltpu.roll(pre, n_rows - 1, 0)
    u = up * cw[0:1] + pre * cw[1:2] + dn * cw[2:3] + cb
    return u[HALO:n_rows - HALO]


def _halo_specs(tm, d, l, rows=HALO):
    nb = tm // rows
    last_blk = l // rows - 1
    return [pl.BlockSpec((1, rows, d), lambda bi, i, *_: (bi, jnp.maximum(i * nb - 1, 0), 0)),
            pl.BlockSpec((1, tm, d), lambda bi, i, *_: (bi, i, 0)),
            pl.BlockSpec((1, rows, d), lambda bi, i, *_: (bi, jnp.minimum((i + 1) * nb, last_blk), 0))]


def _erf(x):
    return lax.erf(x)


def _mix_ffn_kernel(*refs, n_act, alpha, f, tf):
    acts, ws = refs[:3 * n_act], refs[3 * n_act:4 * n_act]
    (xp_ref, x_ref, xn_ref, gate1_ref, g1_ref, b1_ref, sh_ref, sc_ref, gate2_ref,
     wi_ref, cw_ref, cb_ref, wo_ref, g2_ref, b2_ref, o_ref, z_ref) = refs[4 * n_act:]
    i = pl.program_id(1)

    def mixed(k):
        y = _dot(acts[k][0].astype(BF16), ws[0][...])
        for a in range(1, n_act):
            y = y + _dot(acts[3 * a + k][0].astype(BF16), ws[a][...])
        return y

    res_ln = lambda xv, y: _ln(alpha * xv + gate1_ref[0] * y) * g1_ref[...] + b1_ref[...]
    mod = lambda v: _ln(v) * (1.0 + sc_ref[0]) + sh_ref[0]
    x1 = res_ln(x_ref[0], mixed(1))
    hp = mod(res_ln(xp_ref[0], mixed(0)[ACT_HALO - HALO:])) * jnp.where(i == 0, 0.0, 1.0)
    hn = mod(res_ln(xn_ref[0], mixed(2)[:HALO])) * jnp.where(i == pl.num_programs(1) - 1, 0.0, 1.0)
    h = jnp.concatenate([hp, mod(x1), hn], axis=0).astype(BF16)
    for j in range(f // tf):
        a, g = (_dwconv3(_dot(h, wi_ref[:, lo:lo + tf]), cw_ref[:, lo:lo + tf], cb_ref[:, lo:lo + tf])
                for lo in (j * tf, f + j * tf))
        z_ref[:, j * tf:(j + 1) * tf] = (0.5 * a * (1.0 + _erf(a * (2.0 ** -0.5))) * g).astype(BF16)
    y = _dot(z_ref[...], wo_ref[...])
    zz = alpha * x1 + gate2_ref[0] * y
    o_ref[0] = _ln(zz) * g2_ref[...] + b2_ref[...]


def _mix_ffn(acts, w_mix, x, gate1, ln1_g, ln1_b, sh, sc, gate2, w_in, conv_w, conv_b, w_out,
             ln2_g, ln2_b, alpha, name):
    b, l, d = x.shape
    f = w_out.shape[0]
    tm = _pick(l, (512, 256, 128))
    tf = _pick(f, (256, 128))
    vec = lambda bi, i: (bi, 0, 0)
    resident = lambda shape: pl.BlockSpec(shape, lambda bi, i: (0, 0), pipeline_mode=pl.Buffered(1))
    act_specs, act_args = [], []
    for a in acts:
        act_specs += _halo_specs(tm, a.shape[2], l, ACT_HALO)
        act_args += [a, a, a]
    return pl.pallas_call(
        functools.partial(_mix_ffn_kernel, n_act=len(acts), alpha=alpha, f=f, tf=tf),
        grid=(b, l // tm),
        in_specs=act_specs + [resident(w.shape) for w in w_mix] + _halo_specs(tm, d, l, HALO) + [
            pl.BlockSpec((1, 1, d), vec), resident((1, d)), resident((1, d)),
            pl.BlockSpec((1, 1, d), vec), pl.BlockSpec((1, 1, d), vec), pl.BlockSpec((1, 1, d), vec),
            resident((d, 2 * f)), resident((3, 2 * f)), resident((1, 2 * f)), resident((f, d)),
            resident((1, d)), resident((1, d))],
        out_specs=pl.BlockSpec((1, tm, d), lambda bi, i: (bi, i, 0)),
        out_shape=jax.ShapeDtypeStruct((b, l, d), F32),
        scratch_shapes=[pltpu.VMEM((tm, f), BF16)],
        compiler_params=_cp("parallel", "parallel"),
        name=name,
    )(*act_args, *w_mix, x, x, x, gate1, ln1_g.reshape(1, d), ln1_b.reshape(1, d), sh, sc, gate2,
      w_in, conv_w, conv_b.reshape(1, 2 * f), w_out, ln2_g.reshape(1, d), ln2_b.reshape(1, d))


def _hy_inproj_kernel(xp_ref, x_ref, xn_ref, sh_ref, sc_ref, w_ref, cw_ref, cb_ref,
                      x0_ref, vx_ref, *, d, tn):
    i = pl.program_id(1)
    h = _halo_ln(xp_ref, x_ref, xn_ref, sh_ref, sc_ref, i == 0, i == pl.num_programs(1) - 1)
    for c in range(d // tn):
        def part(k):
            lo = k * d + c * tn
            return _dwconv3(_dot(h, w_ref[:, lo:lo + tn]), cw_ref[:, lo:lo + tn], cb_ref[:, lo:lo + tn])
        x0_ref[0, :, c * tn:(c + 1) * tn] = part(0).astype(BF16)
        vx_ref[0, :, c * tn:(c + 1) * tn] = (part(2) * part(1)).astype(BF16)


def _hy_inproj(x, sh, sc, w, conv_w, conv_b):
    b, l, d = x.shape
    tm = _pick(l, (512, 256, 128))
    tn = _pick(d, (256, 128))
    vec = lambda bi, i: (bi, 0, 0)
    const = lambda bi, i: (0, 0)
    return pl.pallas_call(
        functools.partial(_hy_inproj_kernel, d=d, tn=tn),
        grid=(b, l // tm),
        in_specs=_halo_specs(tm, d, l) + [
            pl.BlockSpec((1, 1, d), vec), pl.BlockSpec((1, 1, d), vec),
            pl.BlockSpec((d, 3 * d), const), pl.BlockSpec((3, 3 * d), const),
            pl.BlockSpec((1, 3 * d), const)],
        out_specs=[pl.BlockSpec((1, tm, d), lambda bi, i: (bi, i, 0)),
                   pl.BlockSpec((1, tm, d), lambda bi, i: (bi, i, 0))],
        out_shape=[jax.ShapeDtypeStruct((b, l, d), BF16), jax.ShapeDtypeStruct((b, l, d), BF16)],
        compiler_params=_cp("parallel", "parallel"),
        name="hyena_inproj",
    )(x, x, x, sh, sc, w, conv_w, conv_b.reshape(1, 3 * d))


def _hy_filter_kernel(z_ref, w1_ref, b1_ref, w2_ref, b2_ref, w3_ref, b3_ref, fr_ref, w4_ref,
                      t_ref, dl_ref, hfb_ref, inv_ref, *, d):
    i = pl.program_id(0)
    fr = fr_ref[...]
    hdn = jnp.sin(fr * (_dot3(z_ref[...], w1_ref[...]) + b1_ref[...]))
    hdn = jnp.sin(fr * (_dot3(hdn, w2_ref[...]) + b2_ref[...]))
    hdn = jnp.sin(fr * (_dot3(hdn, w3_ref[...]) + b3_ref[...]))
    h = _dot3(hdn, w4_ref[...])
    decay = jnp.exp(-t_ref[...] * dl_ref[...])
    hf = h[:, :d] * decay
    hb = h[:, d:] * decay
    row = lax.broadcasted_iota(jnp.int32, hb.shape, 0)
    hb = jnp.where((row == 0) & (i == 0), 0.0, hb)
    hfb_ref[0] = hf
    hfb_ref[1] = hb

    @pl.when(i == 0)
    def _():
        inv_ref[...] = jnp.zeros_like(inv_ref)

    inv_ref[...] += (jnp.sum(jnp.abs(hf), axis=0, keepdims=True)
                     + jnp.sum(jnp.abs(hb), axis=0, keepdims=True))

    @pl.when(i == pl.num_programs(0) - 1)
    def _():
        inv_ref[...] = 1.0 / inv_ref[...]


def _hy_filters(l, d, w1, b1, w2, b2, w3, b3, freq, w4):
    hh = w1.shape[1]
    bands = (HY_FILTER_EMB - 1) // 2
    t = np.linspace(0.0, 1.0, l, dtype=np.float32)[:, None]
    w = (2.0 * math.pi * np.arange(l, dtype=np.float32)[:, None] / l).astype(np.float32)
    fb = np.linspace(1e-4, bands - 1, bands, dtype=np.float32)[None, :]
    z = np.concatenate([t, np.cos(fb * w), -np.sin(fb * w)], axis=-1).astype(np.float32)
    zp = np.zeros((l, V7X_LANES), np.float32)
    zp[:, :HY_FILTER_EMB] = z
    w1p = jnp.zeros((V7X_LANES, hh), F32).at[:HY_FILTER_EMB].set(w1.astype(F32))
    min_decay = math.log(HY_DECAY_TARGET) / HY_SLOW_DECAY
    max_decay = math.log(HY_DECAY_TARGET) / HY_FAST_DECAY
    deltas = np.abs(np.linspace(min_decay, max_decay, d, dtype=np.float32))[None, :]
    tl = _pick(l, (512, 256, 128))
    const = lambda i: (0, 0)
    r2 = lambda a: a.astype(F32).reshape(1, -1)
    return pl.pallas_call(
        functools.partial(_hy_filter_kernel, d=d),
        grid=(l // tl,),
        in_specs=[pl.BlockSpec((tl, V7X_LANES), lambda i: (i, 0)),
                  pl.BlockSpec((V7X_LANES, hh), const), pl.BlockSpec((1, hh), const),
                  pl.BlockSpec((hh, hh), const), pl.BlockSpec((1, hh), const),
                  pl.BlockSpec((hh, hh), const), pl.BlockSpec((1, hh), const),
                  pl.BlockSpec((1, hh), const), pl.BlockSpec((hh, 2 * d), const),
                  pl.BlockSpec((tl, 1), lambda i: (i, 0)), pl.BlockSpec((1, d), const)],
        out_specs=[pl.BlockSpec((2, tl, d), lambda i: (0, i, 0)), pl.BlockSpec((1, d), const)],
        out_shape=[jax.ShapeDtypeStruct((2, l, d), F32), jax.ShapeDtypeStruct((1, d), F32)],
        compiler_params=_cp("arbitrary"),
        name="hyena_filter",
    )(jnp.asarray(zp), w1p, r2(b1), w2.astype(F32), r2(b2), w3.astype(F32), r2(b3), r2(freq),
      w4.astype(F32), jnp.asarray(t), jnp.asarray(deltas))


def _hyena_tables(l):
    r = l // DFT_INNER
    n = 2 * l
    k1 = np.arange(2 * r)[None, :, None]
    n1 = np.arange(r)[None, None, :]
    n2 = np.arange(DFT_INNER)[:, None, None]
    th = 2.0 * np.pi * ((k1 * (DFT_INNER * n1 + n2)) % n) / n
    gr, gi = np.cos(th), -np.sin(th)
    g1r = np.concatenate([gr, gi], 1)
    g1c = np.concatenate([np.concatenate([gr, -gi], 2), np.concatenate([gi, gr], 2)], 1)
    k2 = np.arange(DFT_INNER)
    th3 = 2.0 * np.pi * np.outer(k2, k2) / DFT_INNER
    c, s = np.cos(th3), np.sin(th3)
    m3f = np.kron(np.eye(2), np.block([[c, s], [-s, c]]))
    m3i = np.kron(np.eye(2), np.block([[c, -s], [s, c]]))
    thi = np.transpose(th, (0, 2, 1))
    hr, hi = np.cos(thi), np.sin(thi)
    hc = np.concatenate([np.concatenate([hr, -hi], 2), np.concatenate([hi, hr], 2)], 1) / n
    as_bf = lambda a: jnp.asarray(a, dtype=F32).astype(BF16)
    return as_bf(g1r), as_bf(g1c), as_bf(m3f), as_bf(m3i), as_bf(hc)


def _hy1_kernel(v_ref, g_ref, ar_ref, ai_ref, *, r2):
    xa = _swap01(v_ref[0, 0].astype(F32))
    xb = _swap01(v_ref[0, 1].astype(F32))
    for s in range(SUB_BLK):
        z = jnp.concatenate([xa[s], xb[s]], axis=0).astype(BF16)
        a = _dot(g_ref[s], z)
        ar_ref[0, s] = a[:r2].astype(BF16)
        ai_ref[0, s] = a[r2:].astype(BF16)


def _hy1f_kernel(v_ref, g_ref, ar_ref, ai_ref, *, r2):
    x = _swap01(v_ref[0])
    for s in range(SUB_BLK_F32):
        a = _dot(g_ref[s], x[s].astype(BF16))
        ar_ref[0, s] = a[:r2].astype(BF16)
        ai_ref[0, s] = a[r2:].astype(BF16)


def _hy_stage1(v5, g, td, real_input):
    r, d = v5.shape[-3], v5.shape[-1]
    nseq = v5.shape[0]
    if real_input:
        sub = SUB_BLK_F32
        vspec = pl.BlockSpec((1, r, sub, td), lambda p, j, h: (p, 0, j, h))
        body = _hy1f_kernel
    else:
        sub = SUB_BLK
        vspec = pl.BlockSpec((1, 2, r, sub, td), lambda p, j, h: (p, 0, 0, j, h))
        body = _hy1_kernel
    ospec = pl.BlockSpec((1, sub, 2 * r, td), lambda p, j, h: (p, j, 0, h))
    oshape = jax.ShapeDtypeStruct((nseq, DFT_INNER, 2 * r, d), BF16)
    return pl.pallas_call(
        functools.partial(body, r2=2 * r),
        grid=(nseq, DFT_INNER // sub, d // td),
        in_specs=[vspec, pl.BlockSpec((sub,) + g.shape[1:], lambda p, j, h: (j, 0, 0))],
        out_specs=[ospec, ospec],
        out_shape=[oshape, oshape],
        compiler_params=_cp("parallel", "parallel", "parallel"),
        name="hyena_fft_stage1_filter" if real_input else "hyena_fft_stage1",
    )(v5, g)


def _k1_pair_rhs(fr, fi, q):
    return jnp.concatenate([fr[2 * q], fi[2 * q], fr[2 * q + 1], fi[2 * q + 1]], axis=0).astype(BF16)


def _hy2f_kernel(fr_ref, fi_ref, inv_ref, m3f_ref, pm_ref, kr_ref, ki_ref):
    h = DFT_INNER
    frh, fih = _to_lead(fr_ref[0], pm_ref[...]), _to_lead(fi_ref[0], pm_ref[...])
    frb, fib = _to_lead(fr_ref[1], pm_ref[...]), _to_lead(fi_ref[1], pm_ref[...])
    inv = inv_ref[...]
    for q in range(SUB_BLK // 2):
        sh = _dot(m3f_ref[...], _k1_pair_rhs(frh, fih, q))
        sb = _dot(m3f_ref[...], _k1_pair_rhs(frb, fib, q))
        for u in range(2):
            o = 2 * h * u
            kr_ref[2 * q + u] = ((sh[o:o + h] + sb[o:o + h]) * inv).astype(BF16)
            ki_ref[2 * q + u] = ((sh[o + h:o + 2 * h] - sb[o + h:o + 2 * h]) * inv).astype(BF16)


def _hy_filter_spectrum(afr, afi, inv_norm, m3f, pm, td):
    _, _, r2, d = afr.shape
    fspec = pl.BlockSpec((2, DFT_INNER, SUB_BLK, td), lambda j, h: (0, 0, j, h))
    kspec = pl.BlockSpec((SUB_BLK, DFT_INNER, td), lambda j, h: (j, 0, h))
    kshape = jax.ShapeDtypeStruct((r2, DFT_INNER, d), BF16)
    return pl.pallas_call(
        _hy2f_kernel,
        grid=(r2 // SUB_BLK, d // td),
        in_specs=[fspec, fspec, pl.BlockSpec((1, td), lambda j, h: (0, h)),
                  pl.BlockSpec(m3f.shape, lambda j, h: (0, 0)), pl.BlockSpec(pm.shape, lambda j, h: (0, 0))],
        out_specs=[kspec, kspec],
        out_shape=[kshape, kshape],
        compiler_params=_cp("parallel", "parallel"),
        name="hyena_filter_spectrum",
    )(afr, afi, inv_norm, m3f, pm)


def _hy2_kernel(ar_ref, ai_ref, kr_ref, ki_ref, m3f_ref, m3i_ref, pm_ref, br_ref, bi_ref):
    h = DFT_INNER
    ar, ai = _to_lead(ar_ref[0], pm_ref[...]), _to_lead(ai_ref[0], pm_ref[...])
    nq, td = SUB_BLK // 2, ar_ref.shape[3]
    v = _dot(m3f_ref[...], jnp.concatenate([_k1_pair_rhs(ar, ai, q) for q in range(nq)], axis=1))
    cols = []
    for q in range(nq):
        ys = []
        for u in range(2):
            o = 2 * h * u
            vr, vi = v[o:o + h, q * td:(q + 1) * td], v[o + h:o + 2 * h, q * td:(q + 1) * td]
            kr, ki = kr_ref[2 * q + u].astype(F32), ki_ref[2 * q + u].astype(F32)
            ys += [vr * kr - vi * ki, vr * ki + vi * kr]
        cols.append(jnp.concatenate(ys, axis=0).astype(BF16))
    bn = _dot(m3i_ref[...], jnp.concatenate(cols, axis=1))
    brs, bis = [], []
    for q in range(nq):
        bq = bn[:, q * td:(q + 1) * td]
        brs += [bq[0:h], bq[2 * h:3 * h]]
        bis += [bq[h:2 * h], bq[3 * h:4 * h]]
    br_ref[0] = _swap01(jnp.stack(brs, axis=0)).astype(BF16)
    bi_ref[0] = _swap01(jnp.stack(bis, axis=0)).astype(BF16)


def _hy_stage2(ar, ai, kr, ki, m3f, m3i, pm, td):
    p, _, r2, d = ar.shape
    aspec = pl.BlockSpec((1, DFT_INNER, SUB_BLK, td), lambda j, h, pi: (pi, 0, j, h))
    kspec = pl.BlockSpec((SUB_BLK, DFT_INNER, td), lambda j, h, pi: (j, 0, h))
    mspec = pl.BlockSpec(m3f.shape, lambda j, h, pi: (0, 0))
    oshape = jax.ShapeDtypeStruct(ar.shape, BF16)
    return pl.pallas_call(
        _hy2_kernel,
        grid=(r2 // SUB_BLK, d // td, p),
        in_specs=[aspec, aspec, kspec, kspec, mspec, mspec, pl.BlockSpec(pm.shape, lambda j, h, pi: (0, 0))],
        out_specs=[aspec, aspec],
        out_shape=[oshape, oshape],
        compiler_params=_cp("parallel", "parallel", "parallel"),
        name="hyena_fft_stage2",
    )(ar, ai, kr, ki, m3f, m3i, pm)


def _hy3_kernel(br_ref, bi_ref, hc_ref, vx_ref, x0_ref, d_ref, o_ref, *, r):
    ya, yb = [], []
    for s in range(SUB_BLK):
        bb = jnp.concatenate([br_ref[0, s], bi_ref[0, s]], axis=0)
        y = _dot(hc_ref[s], bb)
        ya.append(y[:r])
        yb.append(y[r:])
    for u, ys in enumerate((ya, yb)):
        y = _swap01(jnp.stack(ys, axis=0))
        vx, x0 = vx_ref[0, u].astype(F32), x0_ref[0, u].astype(F32)
        o_ref[0, u] = ((y + vx * d_ref[...]) * x0).astype(o_ref.dtype)


def _hy_stage3(br, bi, hc, vx5, x05, d_skip, td):
    p, _, r, _, d = vx5.shape
    bspec = pl.BlockSpec((1, SUB_BLK, 2 * r, td), lambda pi, j, h: (pi, j, 0, h))
    vspec = pl.BlockSpec((1, 2, r, SUB_BLK, td), lambda pi, j, h: (pi, 0, 0, j, h))
    return pl.pallas_call(
        functools.partial(_hy3_kernel, r=r),
        grid=(p, DFT_INNER // SUB_BLK, d // td),
        in_specs=[bspec, bspec,
                  pl.BlockSpec((SUB_BLK, 2 * r, 4 * r), lambda pi, j, h: (j, 0, 0)),
                  vspec, vspec, pl.BlockSpec((1, 1, td), lambda pi, j, h: (0, 0, h))],
        out_specs=vspec,
        out_shape=jax.ShapeDtypeStruct(vx5.shape, BF16),
        compiler_params=_cp("parallel", "parallel", "parallel"),
        name="hyena_fft_stage3",
    )(br, bi, hc, vx5, x05, d_skip.astype(F32).reshape(1, 1, d))


def _hyena_long_conv(vx, x0, hfb, inv_norm, d_skip, tables):
    b, l, d = vx.shape
    r = l // DFT_INNER
    g1r, g1c, m3f, m3i, hc = tables
    td = _pick(d, (512, 256, 128))
    td_io = _pick(d, (256, 128))
    pair = lambda a: a.reshape(b // 2, 2, r, DFT_INNER, d)
    ar, ai = _hy_stage1(pair(vx), g1c, td_io, real_input=False)
    afr, afi = _hy_stage1(hfb.reshape(2, r, DFT_INNER, d), g1r, td, real_input=True)
    pm, _ = _regroup_perm()
    kr, ki = _hy_filter_spectrum(afr, afi, inv_norm, m3f, pm, td)
    br, bi = _hy_stage2(ar, ai, kr, ki, m3f, m3i, pm, td)
    return _hy_stage3(br, bi, hc, pair(vx), pair(x0), d_skip, td_io).reshape(b, l, d)


def _rope_tables(l):
    rows = l // GRID_W
    axis_dim = DA_HEAD_DIM // 2
    pos_r = jnp.repeat(jnp.arange(rows, dtype=F32), GRID_W)
    pos_c = jnp.tile(jnp.arange(GRID_W, dtype=F32), rows)
    inv = ROPE_THETA ** (-jnp.arange(0, axis_dim, 2, dtype=F32) / axis_dim)
    ar, ac = pos_r[:, None] * inv, pos_c[:, None] * inv
    n = ar.shape[1]
    zeros = jnp.zeros((l, n), F32)
    cos64 = jnp.concatenate([jnp.cos(ar), jnp.cos(ar), jnp.cos(ac), jnp.cos(ac)], -1)
    sa64 = jnp.concatenate([-jnp.sin(ar), zeros, -jnp.sin(ac), zeros], -1)
    sb64 = jnp.concatenate([zeros, jnp.sin(ar), zeros, jnp.sin(ac)], -1)
    rep = V7X_LANES // DA_HEAD_DIM
    return jnp.tile(cos64, (1, rep)), jnp.tile(sa64, (1, rep)), jnp.tile(sb64, (1, rep))


def _lambda_init(layer_idx):
    return 0.8 - 0.6 * math.exp(-0.3 * layer_idx)


def kernel(x, c, ctx, c_ctx, mod_w, mod_b, ln1_g, ln1_b, ln2_g, ln2_b, ffn_w_in, ffn_conv_w, ffn_conv_b, ffn_w_out, da_w_in, da_w_out, da_lam_q1, da_lam_k1, da_lam_q2, da_lam_k2, da_subln_g, hy_w_in, hy_conv_w, hy_conv_b, hy_f_w1, hy_f_b1, hy_f_w2, hy_f_b2, hy_f_w3, hy_f_b3, hy_f_freq, hy_f_w4, hy_d, hy_w_out):
    b, l, d = x.shape
    depth = mod_w.shape[0]
    assert depth == 2 and l % ATT_TK == 0 and d % (2 * V7X_LANES) == 0
    assert b % 2 == 0 and (l // DFT_INNER) % SUB_BLK == 0
    da_w = d // 2
    alpha = (2 * depth) ** 0.25

    rows = -(-(b + 1) // 8) * 8
    cv = jnp.zeros((rows, d), F32).at[:b].set(c.astype(F32)).at[b].set(c_ctx.astype(F32))
    mod = _mod_params(cv, mod_w.astype(F32), mod_b.astype(F32))

    def mods(i):
        m = mod[i].reshape(rows, N_MOD, d)
        return [m[:b, k][:, None, :] for k in range(N_MOD)], [m[b:b + 1, k][:, None, :] for k in range(N_MOD)]

    (sh1, sc1, g1, sh2, sc2, g2), (csh1, csc1, _, _, _, _) = mods(0)
    w_in = da_w_in[0].astype(BF16)
    cos, sa, sb = _rope_tables(l)
    qT, k, vT, f = _inproj(x, sh1, sc1, w_in, cos, sa, sb, da_w)
    kc, vcT = _ctx_kv(ctx, csh1, csc1, w_in[:, da_w:3 * da_w], da_w)
    lam_rows = jnp.zeros((8, V7X_LANES), F32)
    for r_i, p in enumerate((da_lam_q1[0], da_lam_k1[0], da_lam_q2[0], da_lam_k2[0])):
        lam_rows = lam_rows.at[r_i, :DA_HEAD_DIM].set(p.astype(F32))
    o = _diff_attention(qT, k, vT, kc, vcT, lam_rows, da_subln_g[0].astype(F32).reshape(1, -1), _lambda_init(0))
    fm = _fourier_mix(f, _fourier_tables(l, d - da_w))
    w_out = da_w_out[0].astype(BF16)
    x = _mix_ffn([o, fm], [w_out[:da_w], w_out[da_w:]], x, g1, ln1_g[0], ln1_b[0], sh2, sc2, g2,
                 ffn_w_in[0].astype(BF16), ffn_conv_w[0].astype(F32), ffn_conv_b[0].astype(F32),
                 ffn_w_out[0].astype(BF16), ln2_g[0], ln2_b[0], alpha, "l0_mix_ffn")

    (sh1, sc1, g1, sh2, sc2, g2), _ = mods(1)
    x0, vx = _hy_inproj(x, sh1, sc1, hy_w_in[0].astype(BF16), hy_conv_w[0].astype(F32),
                        hy_conv_b[0].astype(F32))
    hfb, inv_norm = _hy_filters(l, d, hy_f_w1[0], hy_f_b1[0], hy_f_w2[0], hy_f_b2[0],
                                hy_f_w3[0], hy_f_b3[0], hy_f_freq[0], hy_f_w4[0])
    z = _hyena_long_conv(vx, x0, hfb, inv_norm, hy_d[0], _hyena_tables(l))
    return _mix_ffn([z], [hy_w_out[0].astype(BF16)], x, g1, ln1_g[1], ln1_b[1], sh2, sc2, g2,
                    ffn_w_in[1].astype(BF16), ffn_conv_w[1].astype(F32), ffn_conv_b[1].astype(F32),
                    ffn_w_out[1].astype(BF16), ln2_g[1], ln2_b[1], alpha, "l1_mix_ffn")
```
